```python
import math
import jax, jax.numpy as jnp
from jax import lax
import numpy as np

D_MODEL = 1024
BATCH = 4
SEQ = 8192
DEPTH = 2

GRID_W = 64
CTX_LEN = 256
D_MIX = D_MODEL
N_MIXERS = 4
GROUP_W = D_MIX // N_MIXERS
N_PARTS = 10
D_IN = N_PARTS * GROUP_W
CONV_A_K = 31
DIFF_HEADS = 4
DIFF_DQK = GROUP_W // (2 * DIFF_HEADS)
DIFF_DV = 2 * DIFF_DQK
ROPE_HALF = DIFF_DQK // 2
ROPE_BASE = 10000.0
Q_BLOCK = 128
CHUNK = 128
SG_GROUPS = 4
SG_DIM = GROUP_W // SG_GROUPS
CONV_D_K = 3
D_FF = 4 * D_MODEL
EPS = 1e-6

kernel_name = 'hybrid_parallel_group_dit_block'


def rms_norm(x, g):
    xf = x.astype(jnp.float32)
    y = xf * lax.rsqrt(jnp.mean(xf * xf, axis=-1, keepdims=True) + EPS)
    return (y * g.astype(jnp.float32)).astype(x.dtype)


def layer_norm(x, g, b):
    xf = x.astype(jnp.float32)
    xc = xf - jnp.mean(xf, axis=-1, keepdims=True)
    y = xc * lax.rsqrt(jnp.mean(xc * xc, axis=-1, keepdims=True) + EPS)
    return (y * g.astype(jnp.float32) + b.astype(jnp.float32)).astype(x.dtype)


def modulate(h, shift, scale):
    return h * (1 + scale) + shift


def depthwise_conv(x, w):
    pad = w.shape[0] // 2
    return lax.conv_general_dilated(
        x, w[:, None, :].astype(x.dtype), window_strides=(1,), padding=[(pad, pad)],
        dimension_numbers=('NWC', 'WIO', 'NWC'), feature_group_count=x.shape[-1])


def axial_rope(rows, dtype):
    row = jnp.repeat(jnp.arange(rows, dtype=jnp.float32), GRID_W)
    col = jnp.tile(jnp.arange(GRID_W, dtype=jnp.float32), rows)
    inv = ROPE_BASE ** (-jnp.arange(0, ROPE_HALF, 2, dtype=jnp.float32) / ROPE_HALF)
    ang_r = row[:, None] * inv
    ang_c = col[:, None] * inv
    return (jnp.cos(ang_r)[:, None, :].astype(dtype), jnp.sin(ang_r)[:, None, :].astype(dtype),
            jnp.cos(ang_c)[:, None, :].astype(dtype), jnp.sin(ang_c)[:, None, :].astype(dtype))


def rotate_pairs(t, cos, sin):
    f = t.shape[-1] // 2
    t1, t2 = t[..., :f], t[..., f:]
    return jnp.concatenate([t1 * cos - t2 * sin, t1 * sin + t2 * cos], axis=-1)


def apply_axial_rope(t, rope):
    cos_r, sin_r, cos_c, sin_c = rope
    return jnp.concatenate([rotate_pairs(t[..., :ROPE_HALF], cos_r, sin_r),
                            rotate_pairs(t[..., ROPE_HALF:], cos_c, sin_c)], axis=-1)


def qk_heads(t, rope):
    b, n, _ = t.shape
    t = t.reshape(b, n, DIFF_HEADS, 2, DIFF_DQK)
    t1, t2 = t[..., 0, :], t[..., 1, :]
    if rope is not None:
        t1, t2 = apply_axial_rope(t1, rope), apply_axial_rope(t2, rope)
    return t1, t2


def v_heads(t):
    b, n, _ = t.shape
    return t.reshape(b, n, DIFF_HEADS, DIFF_DV)


def diff_attention(q1, q2, k1, k2, v, lam, lam_init, subln_g):
    b, n = q1.shape[:2]
    nb = n // Q_BLOCK
    scale = DIFF_DQK ** -0.5

    def blocks(t):
        return t.reshape(b, nb, Q_BLOCK, DIFF_HEADS, DIFF_DQK).transpose(1, 0, 2, 3, 4)

    def one_block(qs):
        qb1, qb2 = qs
        p1 = jax.nn.softmax(jnp.einsum('bqhd,bkhd->bhqk', qb1, k1).astype(jnp.float32) * scale, axis=-1)
        p2 = jax.nn.softmax(jnp.einsum('bqhd,bkhd->bhqk', qb2, k2).astype(jnp.float32) * scale, axis=-1)
        w = (p1 - lam * p2).astype(v.dtype)
        return jnp.einsum('bhqk,bkhd->bqhd', w, v)

    o = lax.map(one_block, (blocks(q1), blocks(q2)))
    o = o.transpose(1, 0, 2, 3, 4).reshape(b, n, DIFF_HEADS, DIFF_DV)
    o = rms_norm(o, subln_g) * (1.0 - lam_init)
    return o.reshape(b, n, GROUP_W)


def conformer_conv(a_val, a_gate, conv_w, conv_b, ln_g, ln_b):
    glu = a_val * jax.nn.sigmoid(a_gate)
    y = depthwise_conv(glu, conv_w) + conv_b
    return jax.nn.silu(layer_norm(y, ln_g, ln_b))


def spatial_gating(u, v, ln_g, ln_b, w_s, b_s):
    u = jax.nn.gelu(u)
    v = layer_norm(jax.nn.gelu(v), ln_g, ln_b)
    b, n, _ = v.shape
    vc = v.reshape(b, n // CHUNK, CHUNK, SG_GROUPS, SG_DIM)
    s = jnp.einsum('gpq,bcqgd->bcpgd', w_s, vc) + b_s.T[:, :, None]
    return u * s.reshape(b, n, GROUP_W)


def short_conv_mixer(bg, cg, xin, conv_w):
    return bg * depthwise_conv(cg * xin, conv_w)


def token_mixers(z, k1, k2, v, rope, lam, lam_init, conv_a_w, conv_a_b, ln_a_g, ln_a_b, subln_g,
                 sg_ln_g, sg_ln_b, sg_w, sg_b, conv_d_w, w_out):
    a_val, a_gate, q, _, _, u, sv, bg, cg, xin = z
    y_a = conformer_conv(a_val, a_gate, conv_a_w, conv_a_b, ln_a_g, ln_a_b)
    q1, q2 = qk_heads(q, rope)
    y_b = diff_attention(q1, q2, k1, k2, v, lam, lam_init, subln_g)
    y_c = spatial_gating(u, sv, sg_ln_g, sg_ln_b, sg_w, sg_b)
    y_d = short_conv_mixer(bg, cg, xin, conv_d_w)
    return jnp.concatenate([y_a, y_b, y_c, y_d], axis=-1) @ w_out


def channel_mixer(h, w1, w2):
    return jnp.square(jax.nn.relu(h @ w1)) @ w2


def setup_inputs(seed: int = 0) -> dict:
    key = jax.random.key(seed)
    ks = jax.random.split(key, 32)

    def nrm(k, shape, scale):
        return jax.random.normal(k, shape, jnp.float32) * scale

    def gain(k, shape):
        return 1.0 + nrm(k, shape, 0.05)

    return {
        'x': nrm(ks[0], (BATCH, SEQ, D_MODEL), 1.0),
        'c': nrm(ks[1], (BATCH, D_MODEL), 1.0),
        'ctx': nrm(ks[2], (BATCH, CTX_LEN, D_MODEL), 1.0),
        'c_ctx': nrm(ks[3], (D_MODEL,), 1.0),
        'ada_w': nrm(ks[4], (DEPTH, D_MODEL, 6 * D_MODEL), 0.5 * D_MODEL ** -0.5),
        'ada_b': nrm(ks[5], (DEPTH, 6 * D_MODEL), 0.01),
        'norm1_g': gain(ks[6], (DEPTH, D_MODEL)),
        'norm2_g': gain(ks[7], (DEPTH, D_MODEL)),
        'w_in': nrm(ks[8], (DEPTH, D_MODEL, D_IN), D_MODEL ** -0.5),
        'conv_a_w': nrm(ks[9], (DEPTH, CONV_A_K, GROUP_W), CONV_A_K ** -0.5),
        'conv_a_b': nrm(ks[10], (DEPTH, GROUP_W), 0.01),
        'ln_a_g': gain(ks[11], (DEPTH, GROUP_W)),
        'ln_a_b': nrm(ks[12], (DEPTH, GROUP_W), 0.01),
        'lam_q1': nrm(ks[13], (DEPTH, DIFF_DQK), 0.1),
        'lam_k1': nrm(ks[14], (DEPTH, DIFF_DQK), 0.1),
        'lam_q2': nrm(ks[15], (DEPTH, DIFF_DQK), 0.1),
        'lam_k2': nrm(ks[16], (DEPTH, DIFF_DQK), 0.1),
        'subln_g': gain(ks[17], (DEPTH, DIFF_DV)),
        'sg_ln_g': gain(ks[18], (DEPTH, GROUP_W)),
        'sg_ln_b': nrm(ks[19], (DEPTH, GROUP_W), 0.01),
        'sg_w': nrm(ks[20], (DEPTH, SG_GROUPS, CHUNK, CHUNK), CHUNK ** -0.5),
        'sg_b': gain(ks[21], (DEPTH, SG_GROUPS, CHUNK)),
        'conv_d_w': nrm(ks[22], (DEPTH, CONV_D_K, GROUP_W), CONV_D_K ** -0.5),
        'w_out': nrm(ks[23], (DEPTH, D_MIX, D_MODEL), D_MIX ** -0.5),
        'mlp_w1': nrm(ks[24], (DEPTH, D_MODEL, D_FF), D_MODEL ** -0.5),
        'mlp_w2': nrm(ks[25], (DEPTH, D_FF, D_MODEL), D_FF ** -0.5),
        'final_g': gain(ks[26], (D_MODEL,)),
    }


def reference(x, c, ctx, c_ctx, ada_w, ada_b, norm1_g, norm2_g, w_in, conv_a_w, conv_a_b, ln_a_g,
              ln_a_b, lam_q1, lam_k1, lam_q2, lam_k2, subln_g, sg_ln_g, sg_ln_b, sg_w, sg_b, conv_d_w,
              w_out, mlp_w1, mlp_w2, final_g):
    ROWS = x.shape[1] // GRID_W
    rope = axial_rope(ROWS, x.dtype)
    h_ctx = ctx
    for l in range(DEPTH):
        lam_init = 0.8 - 0.6 * math.exp(-0.3 * l)
        lam = (jnp.exp(jnp.sum(lam_q1[l] * lam_k1[l]).astype(jnp.float32))
               - jnp.exp(jnp.sum(lam_q2[l] * lam_k2[l]).astype(jnp.float32)) + lam_init)
        mod_x = jnp.split((jax.nn.silu(c) @ ada_w[l] + ada_b[l])[:, None, :], 6, axis=-1)
        mod_c = jnp.split(jax.nn.silu(c_ctx) @ ada_w[l] + ada_b[l], 6, axis=-1)
        mix_args = (rope, lam, lam_init, conv_a_w[l], conv_a_b[l], ln_a_g[l], ln_a_b[l], subln_g[l],
                    sg_ln_g[l], sg_ln_b[l], sg_w[l], sg_b[l], conv_d_w[l], w_out[l])

        hc = modulate(rms_norm(h_ctx, norm1_g[l]), mod_c[0], mod_c[1])
        zc = jnp.split(hc @ w_in[l], N_PARTS, axis=-1)
        k1c, k2c = qk_heads(zc[3], None)
        vc = v_heads(zc[4])

        hx = modulate(rms_norm(x, norm1_g[l]), mod_x[0], mod_x[1])
        zx = jnp.split(hx @ w_in[l], N_PARTS, axis=-1)
        k1x, k2x = qk_heads(zx[3], rope)
        k1 = jnp.concatenate([k1c, k1x], axis=1)
        k2 = jnp.concatenate([k2c, k2x], axis=1)
        v = jnp.concatenate([vc, v_heads(zx[4])], axis=1)
        x = x + mod_x[2] * token_mixers(zx, k1, k2, v, *mix_args)
        x = x + mod_x[5] * channel_mixer(modulate(rms_norm(x, norm2_g[l]), mod_x[3], mod_x[4]),
                                         mlp_w1[l], mlp_w2[l])

        if l < DEPTH - 1:
            ctx_args = (None,) + mix_args[1:]
            h_ctx = h_ctx + mod_c[2] * token_mixers(zc, k1c, k2c, vc, *ctx_args)
            h_ctx = h_ctx + mod_c[5] * channel_mixer(
                modulate(rms_norm(h_ctx, norm2_g[l]), mod_c[3], mod_c[4]), mlp_w1[l], mlp_w2[l])
    return rms_norm(x, final_g)
```

```python
import functools
import math

import jax
import jax.numpy as jnp
from jax import lax
from jax.experimental import pallas as pl
from jax.experimental.pallas import tpu as pltpu

F32 = jnp.float32
BF16 = jnp.bfloat16

GRID_W = 64
GROUP_W = 256
CONV_A_K = 31
CONV_D_K = 3
DIFF_HEADS = 4
DIFF_DQK = 32
DIFF_DV = 64
ROPE_HALF = DIFF_DQK // 2
ROPE_FREQS = ROPE_HALF // 2
ROPE_BASE = 10000.0
CHUNK = 128
SG_GROUPS = 4
SG_DIM = GROUP_W // SG_GROUPS
EPS = 1e-6

HALO = 16
V_ROWS = 80
NEG_BIG = -1e30
VMEM_LIMIT = 56 * 1024 * 1024


def _silu(x):
    return x * jax.nn.sigmoid(x)


def _layer_norm(x, g, b):
    xc = x - jnp.mean(x, axis=-1, keepdims=True)
    y = xc * lax.rsqrt(jnp.mean(xc * xc, axis=-1, keepdims=True) + EPS)
    return y * g + b


def _rms_mod(x, g, shift, scale):
    y = x * lax.rsqrt(jnp.mean(x * x, axis=-1, keepdims=True) + EPS)
    return (y * g) * (1.0 + scale) + shift


def _mod_kernel(cc_ref, w_ref, b_ref, o_ref):
    o_ref[...] = jnp.dot(_silu(cc_ref[...]), w_ref[...], preferred_element_type=F32) + b_ref[...]


def _modulation(cc, ada_w, ada_b):
    depth, d, d6 = ada_w.shape
    out = pl.pallas_call(
        _mod_kernel,
        grid=(depth, d6 // d),
        in_specs=[pl.BlockSpec((8, d), lambda l, j: (0, 0)),
                  pl.BlockSpec((None, d, d), lambda l, j: (l, 0, j)),
                  pl.BlockSpec((None, 1, d), lambda l, j: (l, 0, j))],
        out_specs=pl.BlockSpec((None, 8, d), lambda l, j: (l, 0, j)),
        out_shape=jax.ShapeDtypeStruct((depth, 8, d6), F32),
        name="ada_mod",
    )(cc, ada_w, ada_b.reshape(depth, 1, d6))
    return out.reshape(depth, 8, d6 // d, d)


def _inmix_kernel(x_ref, xp_ref, xn_ref, mod_ref, g1_ref, wh_ref, wr_ref, wt_ref,
                  caw_ref, cab_ref, lag_ref, lab_ref, sgg_ref, sgb_ref, sgw_ref, sgbias_ref, cdw_ref,
                  cos_ref, sina_ref, sinb_ref, cost_ref, sint_ref,
                  yacd_ref, k_ref, qt_ref, vt_ref, glu_scr, p_scr, *, tm, n_tok):
    i = pl.program_id(1)
    shift = mod_ref[0:1, :]
    scale = mod_ref[1:2, :]
    g1 = g1_ref[...]
    h = _rms_mod(x_ref[...], g1, shift, scale).astype(BF16)
    hp = _rms_mod(xp_ref[...], g1, shift, scale).astype(BF16)
    hn = _rms_mod(xn_ref[...], g1, shift, scale).astype(BF16)
    h_ext = jnp.concatenate([hp, h, hn], axis=0)

    zh = jnp.dot(h_ext, wh_ref[...], preferred_element_type=F32)
    zr = jnp.dot(h, wr_ref[...], preferred_element_type=F32)
    zt = lax.dot_general(wt_ref[...], h, (((1,), (1,)), ((), ())),
                         preferred_element_type=F32)

    pos = i * tm - HALO + lax.broadcasted_iota(jnp.int32, (tm + 2 * HALO, 1), 0)
    valid = jnp.logical_and(pos >= 0, pos < n_tok)
    a_val, a_gate = zh[:, 0:GROUP_W], zh[:, GROUP_W:2 * GROUP_W]
    cg, xin = zh[:, 2 * GROUP_W:3 * GROUP_W], zh[:, 3 * GROUP_W:4 * GROUP_W]
    glu_scr[...] = jnp.where(valid, a_val * jax.nn.sigmoid(a_gate), 0.0)
    p_scr[...] = jnp.where(valid, cg * xin, 0.0)

    acc = jnp.zeros((tm, GROUP_W), F32) + cab_ref[...]
    for j in range(CONV_A_K):
        off = HALO - CONV_A_K // 2 + j
        acc = acc + caw_ref[j:j + 1, :] * glu_scr[off:off + tm, :]
    y_a = _silu(_layer_norm(acc, lag_ref[...], lab_ref[...]))

    conv_d = jnp.zeros((tm, GROUP_W), F32)
    for j in range(CONV_D_K):
        off = HALO - CONV_D_K // 2 + j
        conv_d = conv_d + cdw_ref[j:j + 1, :] * p_scr[off:off + tm, :]
    y_d = zr[:, 3 * GROUP_W:4 * GROUP_W] * conv_d

    u = jax.nn.gelu(zr[:, GROUP_W:2 * GROUP_W])
    sv = _layer_norm(jax.nn.gelu(zr[:, 2 * GROUP_W:3 * GROUP_W]), sgg_ref[...], sgb_ref[...]).astype(BF16)
    lane_group = lax.broadcasted_iota(jnp.int32, (CHUNK, GROUP_W), 1) // SG_DIM
    gated = []
    for c in range(tm // CHUNK):
        vc = sv[c * CHUNK:(c + 1) * CHUNK, :]
        s = sgbias_ref[...]
        for g in range(SG_GROUPS):
            sg = jnp.dot(sgw_ref[g], vc, preferred_element_type=F32)
            s = s + jnp.where(lane_group == g, sg, 0.0)
        gated.append(s)
    y_c = u * jnp.concatenate(gated, axis=0)

    yacd_ref[...] = jnp.concatenate([y_a, y_c, y_d], axis=-1).astype(BF16)

    k = zr[:, 0:GROUP_W]
    k = (k * cos_ref[...] + pltpu.roll(k, GROUP_W - ROPE_FREQS, 1) * sina_ref[...]
         + pltpu.roll(k, ROPE_FREQS, 1) * sinb_ref[...])
    k_ref[...] = k.astype(BF16)

    qt = zt[0:GROUP_W, :]
    cos_t, sin_t = cost_ref[...], sint_ref[...]
    rows = []
    for grp in range(GROUP_W // ROPE_HALF):
        lo = qt[grp * ROPE_HALF:grp * ROPE_HALF + ROPE_FREQS, :]
        hi = qt[grp * ROPE_HALF + ROPE_FREQS:(grp + 1) * ROPE_HALF, :]
        axis = grp % 2
        cs = cos_t[axis * ROPE_FREQS:(axis + 1) * ROPE_FREQS, :]
        sn = sin_t[axis * ROPE_FREQS:(axis + 1) * ROPE_FREQS, :]
        rows.append(lo * cs - hi * sn)
        rows.append(lo * sn + hi * cs)
    qt_ref[...] = (jnp.concatenate(rows, axis=0) * (DIFF_DQK ** -0.5)).astype(BF16)

    pad_rows = jnp.where(lax.broadcasted_iota(jnp.int32, (V_ROWS - DIFF_DV, tm), 0) == 0, 1.0, 0.0)
    pieces = []
    for hd in range(DIFF_HEADS):
        pieces.append(zt[GROUP_W + hd * DIFF_DV:GROUP_W + (hd + 1) * DIFF_DV, :])
        pieces.append(pad_rows)
    vt_ref[...] = jnp.concatenate(pieces, axis=0).astype(BF16)


def _inmix(x, mod, mod_row, g1, wh, wr, wt, caw, cab, lag, lab, sgg, sgb, sgw, sgbias, cdw,
           rope_tabs, tm):
    bsz, n_tok, d = x.shape
    nt = n_tok // tm
    hb = tm // HALO
    n_hblk = n_tok // HALO
    cos_n, sina_n, sinb_n, cos_t, sin_t = rope_tabs
    if mod_row is None:
        mod_map = lambda b, i: (b, 0, 0)
    else:
        mod_map = lambda b, i: (mod_row, 0, 0)
    const2 = lambda b, i: (0, 0)
    tok_map = lambda b, i: (i, 0)
    in_specs = [
        pl.BlockSpec((None, tm, d), lambda b, i: (b, i, 0)),
        pl.BlockSpec((None, HALO, d), lambda b, i: (b, jnp.maximum(i * hb - 1, 0), 0)),
        pl.BlockSpec((None, HALO, d), lambda b, i: (b, jnp.minimum((i + 1) * hb, n_hblk - 1), 0)),
        pl.BlockSpec((None, 6, d), mod_map),
        pl.BlockSpec((1, d), const2),
        pl.BlockSpec(wh.shape, const2),
        pl.BlockSpec(wr.shape, const2),
        pl.BlockSpec(wt.shape, const2),
        pl.BlockSpec(caw.shape, const2),
        pl.BlockSpec(cab.shape, const2),
        pl.BlockSpec(lag.shape, const2),
        pl.BlockSpec(lab.shape, const2),
        pl.BlockSpec(sgg.shape, const2),
        pl.BlockSpec(sgb.shape, const2),
        pl.BlockSpec(sgw.shape, lambda b, i: (0, 0, 0)),
        pl.BlockSpec(sgbias.shape, const2),
        pl.BlockSpec(cdw.shape, const2),
        pl.BlockSpec((tm, GROUP_W), tok_map),
        pl.BlockSpec((tm, GROUP_W), tok_map),
        pl.BlockSpec((tm, GROUP_W), tok_map),
        pl.BlockSpec((2 * ROPE_FREQS, tm), lambda b, i: (0, i)),
        pl.BlockSpec((2 * ROPE_FREQS, tm), lambda b, i: (0, i)),
    ]
    out_specs = [
        pl.BlockSpec((None, tm, 3 * GROUP_W), lambda b, i: (b, i, 0)),
        pl.BlockSpec((None, tm, GROUP_W), lambda b, i: (b, i, 0)),
        pl.BlockSpec((None, GROUP_W, tm), lambda b, i: (b, 0, i)),
        pl.BlockSpec((None, DIFF_HEADS * V_ROWS, tm), lambda b, i: (b, 0, i)),
    ]
    out_shape = [
        jax.ShapeDtypeStruct((bsz, n_tok, 3 * GROUP_W), BF16),
        jax.ShapeDtypeStruct((bsz, n_tok, GROUP_W), BF16),
        jax.ShapeDtypeStruct((bsz, GROUP_W, n_tok), BF16),
        jax.ShapeDtypeStruct((bsz, DIFF_HEADS * V_ROWS, n_tok), BF16),
    ]
    yacd, k, qt, vt = pl.pallas_call(
        functools.partial(_inmix_kernel, tm=tm, n_tok=n_tok),
        grid=(bsz, nt),
        in_specs=in_specs,
        out_specs=out_specs,
        out_shape=out_shape,
        scratch_shapes=[pltpu.VMEM((tm + 2 * HALO, GROUP_W), F32),
                        pltpu.VMEM((tm + 2 * HALO, GROUP_W), F32)],
        compiler_params=pltpu.CompilerParams(
            dimension_semantics=("arbitrary", "arbitrary"), vmem_limit_bytes=VMEM_LIMIT),
        name="inmix",
    )(x, x, x, mod, g1, wh, wr, wt, caw, cab, lag, lab, sgg, sgb, sgw, sgbias, cdw,
      cos_n, sina_n, sinb_n, cos_t, sin_t)
    return yacd, k, qt, vt.reshape(bsz, DIFF_HEADS, V_ROWS, n_tok)


def _attn_kernel(*refs, tq, tk, n_keys, has_ctx, lam_init):
    lq1_ref, lk1_ref, lq2_ref, lk2_ref, g_ref, qt_ref, k_ref, vt_ref = refs[:8]
    if has_ctx:
        kc_ref, vtc_ref, o_ref = refs[8:]
    else:
        (o_ref,) = refs[8:]
    hd = pl.program_id(1)

    qt = qt_ref[...]
    comp = lax.broadcasted_iota(jnp.int32, (GROUP_W, 1), 0) // DIFF_DQK
    qm = [jnp.where(comp == 2 * hd + c, qt, jnp.zeros_like(qt)) for c in range(2)]

    def step(kt, vt, carry):
        out = []
        for c in range(2):
            m_old, acc = carry[c]
            s = jnp.dot(kt, qm[c], preferred_element_type=F32)
            m_new = jnp.maximum(m_old, jnp.max(s, axis=0, keepdims=True))
            alpha = jnp.exp(m_old - m_new)
            e = jnp.exp(s - m_new).astype(BF16)
            acc = alpha * acc + jnp.dot(vt, e, preferred_element_type=F32)
            out.append((m_new, acc))
        return tuple(out)

    init = tuple((jnp.full((1, tq), NEG_BIG, F32), jnp.zeros((V_ROWS, tq), F32)) for _ in range(2))
    carry = init
    if has_ctx:
        carry = step(kc_ref[...], vtc_ref[...], carry)

    def body(t, carry):
        start = pl.multiple_of(t * tk, tk)
        return step(k_ref[pl.ds(start, tk), :], vt_ref[:, pl.ds(start, tk)], carry)

    carry = lax.fori_loop(0, n_keys // tk, body, carry)

    lam = (jnp.exp(jnp.sum(lq1_ref[...] * lk1_ref[...], axis=-1, keepdims=True))
           - jnp.exp(jnp.sum(lq2_ref[...] * lk2_ref[...], axis=-1, keepdims=True)) + lam_init)
    (_, acc1), (_, acc2) = carry
    o1 = acc1[0:DIFF_DV, :] / acc1[DIFF_DV:DIFF_DV + 1, :]
    o2 = acc2[0:DIFF_DV, :] / acc2[DIFF_DV:DIFF_DV + 1, :]
    o = o1 - lam * o2
    y = o * lax.rsqrt(jnp.mean(o * o, axis=0, keepdims=True) + EPS)
    o_ref[...] = ((y * g_ref[...]) * (1.0 - lam_init)).astype(o_ref.dtype)


def _attention(lam_params, subln_col, qt, k, vt, ctx_kv, lam_init, tq, tk):
    bsz, _, n_q = qt.shape
    n_keys = k.shape[1]
    has_ctx = ctx_kv is not None
    const2 = lambda b, h, i: (0, 0)
    in_specs = [pl.BlockSpec((1, DIFF_DQK), const2)] * 4 + [
        pl.BlockSpec((DIFF_DV, 1), const2),
        pl.BlockSpec((None, GROUP_W, tq), lambda b, h, i: (b, 0, i)),
        pl.BlockSpec((None, n_keys, GROUP_W), lambda b, h, i: (b, 0, 0)),
        pl.BlockSpec((None, None, V_ROWS, n_keys), lambda b, h, i: (b, h, 0, 0)),
    ]
    args = list(lam_params) + [subln_col, qt, k, vt]
    if has_ctx:
        kc, vtc = ctx_kv
        n_ctx = kc.shape[1]
        in_specs += [pl.BlockSpec((None, n_ctx, GROUP_W), lambda b, h, i: (b, 0, 0)),
                     pl.BlockSpec((None, None, V_ROWS, n_ctx), lambda b, h, i: (b, h, 0, 0))]
        args += [kc, vtc]
    return pl.pallas_call(
        functools.partial(_attn_kernel, tq=tq, tk=tk, n_keys=n_keys, has_ctx=has_ctx, lam_init=lam_init),
        grid=(bsz, DIFF_HEADS, n_q // tq),
        in_specs=in_specs,
        out_specs=pl.BlockSpec((None, DIFF_DV, tq), lambda b, h, i: (b, h, i)),
        out_shape=jax.ShapeDtypeStruct((bsz, GROUP_W, n_q), BF16),
        compiler_params=pltpu.CompilerParams(
            dimension_semantics=("arbitrary", "arbitrary", "arbitrary"), vmem_limit_bytes=VMEM_LIMIT),
        name="diff_attn",
    )(*args)


def _outmlp_kernel(x_ref, yacd_ref, yt_ref, mod_ref, g2_ref, woacd_ref, wob_ref, w1_ref, w2_ref, fg_ref,
                   o_ref, *, ff_chunk, final_norm):
    gate_mix = mod_ref[2:3, :]
    shift, scale, gate_mlp = mod_ref[3:4, :], mod_ref[4:5, :], mod_ref[5:6, :]
    mix = jnp.dot(yacd_ref[...], woacd_ref[...], preferred_element_type=F32)
    mix = mix + lax.dot_general(yt_ref[...], wob_ref[...], (((0,), (0,)), ((), ())),
                                preferred_element_type=F32)
    x1 = x_ref[...] + gate_mix * mix
    h = _rms_mod(x1, g2_ref[...], shift, scale).astype(BF16)
    d_ff = w1_ref.shape[1]
    mlp = jnp.zeros_like(x1)
    for j in range(d_ff // ff_chunk):
        t = jnp.dot(h, w1_ref[:, j * ff_chunk:(j + 1) * ff_chunk], preferred_element_type=F32)
        t = jnp.square(jnp.maximum(t, 0.0)).astype(BF16)
        mlp = mlp + jnp.dot(t, w2_ref[j * ff_chunk:(j + 1) * ff_chunk, :], preferred_element_type=F32)
    x2 = x1 + gate_mlp * mlp
    if final_norm:
        x2 = x2 * lax.rsqrt(jnp.mean(x2 * x2, axis=-1, keepdims=True) + EPS) * fg_ref[...]
    o_ref[...] = x2


def _outmlp(x, yacd, yt, mod, mod_row, g2, woacd, wob, w1, w2, fg, tm, final_norm):
    bsz, n_tok, d = x.shape
    if mod_row is None:
        mod_map = lambda b, i: (b, 0, 0)
    else:
        mod_map = lambda b, i: (mod_row, 0, 0)
    const2 = lambda b, i: (0, 0)
    return pl.pallas_call(
        functools.partial(_outmlp_kernel, ff_chunk=1024, final_norm=final_norm),
        grid=(bsz, n_tok // tm),
        in_specs=[
            pl.BlockSpec((None, tm, d), lambda b, i: (b, i, 0)),
            pl.BlockSpec((None, tm, 3 * GROUP_W), lambda b, i: (b, i, 0)),
            pl.BlockSpec((None, GROUP_W, tm), lambda b, i: (b, 0, i)),
            pl.BlockSpec((None, 6, d), mod_map),
            pl.BlockSpec((1, d), const2),
            pl.BlockSpec(woacd.shape, const2),
            pl.BlockSpec(wob.shape, const2),
            pl.BlockSpec(w1.shape, const2),
            pl.BlockSpec(w2.shape, const2),
            pl.BlockSpec((1, d), const2),
        ],
        out_specs=pl.BlockSpec((None, tm, d), lambda b, i: (b, i, 0)),
        out_shape=jax.ShapeDtypeStruct((bsz, n_tok, d), F32),
        compiler_params=pltpu.CompilerParams(
            dimension_semantics=("arbitrary", "arbitrary"), vmem_limit_bytes=VMEM_LIMIT),
        name="outmlp",
    )(x, yacd, yt, mod, g2, woacd, wob, w1, w2, fg)


def _rope_tables(n_tok, identity):
    if identity:
        cos_r = cos_c = jnp.ones((n_tok, ROPE_FREQS), F32)
        sin_r = sin_c = jnp.zeros((n_tok, ROPE_FREQS), F32)
    else:
        rows = n_tok // GRID_W
        row = jnp.repeat(jnp.arange(rows, dtype=F32), GRID_W)
        col = jnp.tile(jnp.arange(GRID_W, dtype=F32), rows)
        inv = ROPE_BASE ** (-jnp.arange(0, ROPE_HALF, 2, dtype=F32) / ROPE_HALF)
        ang_r = row[:, None] * inv
        ang_c = col[:, None] * inv
        cos_r, sin_r, cos_c, sin_c = jnp.cos(ang_r), jnp.sin(ang_r), jnp.cos(ang_c), jnp.sin(ang_c)
    zero = jnp.zeros_like(sin_r)
    reps = GROUP_W // DIFF_DQK
    cos_n = jnp.tile(jnp.concatenate([cos_r, cos_r, cos_c, cos_c], axis=1), (1, reps))
    sina_n = jnp.tile(jnp.concatenate([-sin_r, zero, -sin_c, zero], axis=1), (1, reps))
    sinb_n = jnp.tile(jnp.concatenate([zero, sin_r, zero, sin_c], axis=1), (1, reps))
    cos_t = jnp.concatenate([cos_r, cos_c], axis=1).T
    sin_t = jnp.concatenate([sin_r, sin_c], axis=1).T
    return cos_n, sina_n, sinb_n, cos_t, sin_t


def kernel(x, c, ctx, c_ctx, ada_w, ada_b, norm1_g, norm2_g, w_in, conv_a_w, conv_a_b, ln_a_g, ln_a_b,
           lam_q1, lam_k1, lam_q2, lam_k2, subln_g, sg_ln_g, sg_ln_b, sg_w, sg_b, conv_d_w, w_out,
           mlp_w1, mlp_w2, final_g):
    bsz, n_tok, d = x.shape
    n_ctx = ctx.shape[1]
    depth = ada_w.shape[0]
    tm = min(512, n_tok)
    tm_ctx = min(512, n_ctx)
    tq = 256
    tk = min(512, n_tok)
    ctx_row = bsz

    cc = jnp.zeros((8, d), F32).at[:bsz].set(c).at[ctx_row].set(c_ctx)
    mod = _modulation(cc, ada_w, ada_b)

    rope_x = _rope_tables(n_tok, identity=False)
    rope_c = _rope_tables(n_ctx, identity=True)
    row2 = lambda a: a.reshape(1, -1)

    h_ctx = ctx
    for l in range(depth):
        lam_init = 0.8 - 0.6 * math.exp(-0.3 * l)
        parts = [w_in[l][:, p * GROUP_W:(p + 1) * GROUP_W] for p in range(10)]
        wh = jnp.concatenate([parts[0], parts[1], parts[8], parts[9]], axis=1).astype(BF16)
        wr = jnp.concatenate([parts[3], parts[5], parts[6], parts[7]], axis=1).astype(BF16)
        wt = jnp.concatenate([parts[2], parts[4]], axis=1).T.astype(BF16)
        sgbias = jnp.repeat(sg_b[l].T, SG_DIM, axis=1)
        mix_w = (row2(norm1_g[l]), wh, wr, wt, conv_a_w[l], row2(conv_a_b[l]), row2(ln_a_g[l]),
                 row2(ln_a_b[l]), row2(sg_ln_g[l]), row2(sg_ln_b[l]), sg_w[l].astype(BF16), sgbias,
                 conv_d_w[l])
        lam_params = (row2(lam_q1[l]), row2(lam_k1[l]), row2(lam_q2[l]), row2(lam_k2[l]))
        subln_col = subln_g[l].reshape(DIFF_DV, 1)
        woacd = jnp.concatenate([w_out[l][0:GROUP_W], w_out[l][2 * GROUP_W:]], axis=0).astype(BF16)
        wob = w_out[l][GROUP_W:2 * GROUP_W].astype(BF16)
        w1 = mlp_w1[l].astype(BF16)
        w2 = mlp_w2[l].astype(BF16)
        g2 = row2(norm2_g[l])
        fg = row2(final_g)
        last = l == depth - 1

        yacd_c, k_c, qt_c, vt_c = _inmix(h_ctx, mod[l], ctx_row, *mix_w, rope_c, tm_ctx)
        yacd_x, k_x, qt_x, vt_x = _inmix(x, mod[l], None, *mix_w, rope_x, tm)
        yt_x = _attention(lam_params, subln_col, qt_x, k_x, vt_x, (k_c, vt_c), lam_init, tq, tk)
        x = _outmlp(x, yacd_x, yt_x, mod[l], None, g2, woacd, wob, w1, w2, fg, tm, final_norm=last)
        if not last:
            yt_c = _attention(lam_params, subln_col, qt_c, k_c, vt_c, None, lam_init,
                              min(tq, n_ctx), min(tk, n_ctx))
            h_ctx = _outmlp(h_ctx, yacd_c, yt_c, mod[l], ctx_row, g2, woacd, wob, w1, w2, fg, tm_ctx,
                            final_norm=False)
    return x
```

```python
import functools
import math

import jax
import jax.numpy as jnp
from jax import lax
from jax.experimental import pallas as pl
from jax.experimental.pallas import tpu as pltpu

F32 = jnp.float32
BF16 = jnp.bfloat16

GRID_W = 64
GROUP_W = 256
CONV_A_K = 31
CONV_D_K = 3
DIFF_HEADS = 4
DIFF_DQK = 32
DIFF_DV = 64
ROPE_HALF = DIFF_DQK // 2
ROPE_FREQS = ROPE_HALF // 2
ROPE_BASE = 10000.0
CHUNK = 128
SG_GROUPS = 4
SG_DIM = GROUP_W // SG_GROUPS
EPS = 1e-6

HALO = 16
V_ROWS = 80
NEG_BIG = -1e30
Q_SCALE = DIFF_DQK ** -0.5 * math.log2(math.e)
VMEM_LIMIT = 56 * 1024 * 1024


def _silu(x):
    return x * jax.nn.sigmoid(x)


def _layer_norm(x, g, b):
    xc = x - jnp.mean(x, axis=-1, keepdims=True)
    y = xc * lax.rsqrt(jnp.mean(xc * xc, axis=-1, keepdims=True) + EPS)
    return y * g + b


def _rms_mod(x, g, shift, scale):
    y = x * lax.rsqrt(jnp.mean(x * x, axis=-1, keepdims=True) + EPS)
    return (y * g) * (1.0 + scale) + shift


def _mod_kernel(cc_ref, w_ref, b_ref, o_ref):
    o_ref[...] = jnp.dot(_silu(cc_ref[...]), w_ref[...], preferred_element_type=F32) + b_ref[...]


def _modulation(cc, ada_w, ada_b):
    depth, d, d6 = ada_w.shape
    out = pl.pallas_call(
        _mod_kernel,
        grid=(depth, d6 // d),
        in_specs=[pl.BlockSpec((8, d), lambda l, j: (0, 0)),
                  pl.BlockSpec((None, d, d), lambda l, j: (l, 0, j)),
                  pl.BlockSpec((None, 1, d), lambda l, j: (l, 0, j))],
        out_specs=pl.BlockSpec((None, 8, d), lambda l, j: (l, 0, j)),
        out_shape=jax.ShapeDtypeStruct((depth, 8, d6), F32),
        name="ada_mod",
    )(cc, ada_w, ada_b.reshape(depth, 1, d6))
    return out.reshape(depth, 8, d6 // d, d)


def _inmix_kernel(x_ref, xp_ref, xn_ref, mod_ref, g1_ref, wh_ref, wr_ref, wt_ref,
                  caw_ref, cab_ref, lag_ref, lab_ref, sgg_ref, sgb_ref, sgw_ref, sgbias_ref, cdw_ref,
                  cos_ref, sina_ref, sinb_ref, cost_ref, sint_ref,
                  yacd_ref, k_ref, qt_ref, vt_ref, glu_scr, p_scr, *, tm, n_tok):
    i = pl.program_id(1)
    shift = mod_ref[0:1, :]
    scale = mod_ref[1:2, :]
    g1 = g1_ref[...]
    h = _rms_mod(x_ref[...], g1, shift, scale).astype(BF16)
    hp = _rms_mod(xp_ref[...], g1, shift, scale).astype(BF16)
    hn = _rms_mod(xn_ref[...], g1, shift, scale).astype(BF16)
    h_ext = jnp.concatenate([hp, h, hn], axis=0)

    zh = jnp.dot(h_ext, wh_ref[...], preferred_element_type=F32)
    zr = jnp.dot(h, wr_ref[...], preferred_element_type=F32)
    zt = lax.dot_general(wt_ref[...], h, (((1,), (1,)), ((), ())),
                         preferred_element_type=F32)

    pos = i * tm - HALO + lax.broadcasted_iota(jnp.int32, (tm + 2 * HALO, 1), 0)
    valid = jnp.logical_and(pos >= 0, pos < n_tok)
    a_val, a_gate = zh[:, 0:GROUP_W], zh[:, GROUP_W:2 * GROUP_W]
    cg, xin = zh[:, 2 * GROUP_W:3 * GROUP_W], zh[:, 3 * GROUP_W:4 * GROUP_W]
    glu_scr[...] = jnp.where(valid, a_val * jax.nn.sigmoid(a_gate), 0.0)
    p_scr[...] = jnp.where(valid, cg * xin, 0.0)

    acc = jnp.zeros((tm, GROUP_W), F32) + cab_ref[...]
    for j in range(CONV_A_K):
        off = HALO - CONV_A_K // 2 + j
        acc = acc + caw_ref[j:j + 1, :] * glu_scr[off:off + tm, :]
    y_a = _silu(_layer_norm(acc, lag_ref[...], lab_ref[...]))

    conv_d = jnp.zeros((tm, GROUP_W), F32)
    for j in range(CONV_D_K):
        off = HALO - CONV_D_K // 2 + j
        conv_d = conv_d + cdw_ref[j:j + 1, :] * p_scr[off:off + tm, :]
    y_d = zr[:, 3 * GROUP_W:4 * GROUP_W] * conv_d

    u = jax.nn.gelu(zr[:, GROUP_W:2 * GROUP_W])
    sv = _layer_norm(jax.nn.gelu(zr[:, 2 * GROUP_W:3 * GROUP_W]), sgg_ref[...], sgb_ref[...]).astype(BF16)
    lane_group = lax.broadcasted_iota(jnp.int32, (CHUNK, GROUP_W), 1) // SG_DIM
    gated = []
    for c in range(tm // CHUNK):
        vc = sv[c * CHUNK:(c + 1) * CHUNK, :]
        s = sgbias_ref[...]
        for g in range(SG_GROUPS):
            sg = jnp.dot(sgw_ref[g], vc, preferred_element_type=F32)
            s = s + jnp.where(lane_group == g, sg, 0.0)
        gated.append(s)
    y_c = u * jnp.concatenate(gated, axis=0)

    yacd_ref[...] = jnp.concatenate([y_a, y_c, y_d], axis=-1).astype(BF16)

    k = zr[:, 0:GROUP_W]
    k = (k * cos_ref[...] + pltpu.roll(k, GROUP_W - ROPE_FREQS, 1) * sina_ref[...]
         + pltpu.roll(k, ROPE_FREQS, 1) * sinb_ref[...])
    k_ref[...] = k.astype(BF16)

    qt = zt[0:GROUP_W, :]
    cos_t, sin_t = cost_ref[...], sint_ref[...]
    rows = []
    for grp in range(GROUP_W // ROPE_HALF):
        lo = qt[grp * ROPE_HALF:grp * ROPE_HALF + ROPE_FREQS, :]
        hi = qt[grp * ROPE_HALF + ROPE_FREQS:(grp + 1) * ROPE_HALF, :]
        axis = grp % 2
        cs = cos_t[axis * ROPE_FREQS:(axis + 1) * ROPE_FREQS, :]
        sn = sin_t[axis * ROPE_FREQS:(axis + 1) * ROPE_FREQS, :]
        rows.append(lo * cs - hi * sn)
        rows.append(lo * sn + hi * cs)
    qt_ref[...] = (jnp.concatenate(rows, axis=0) * Q_SCALE).astype(BF16)

    pad_rows = jnp.where(lax.broadcasted_iota(jnp.int32, (V_ROWS - DIFF_DV, tm), 0) == 0, 1.0, 0.0)
    pieces = []
    for hd in range(DIFF_HEADS):
        pieces.append(zt[GROUP_W + hd * DIFF_DV:GROUP_W + (hd + 1) * DIFF_DV, :])
        pieces.append(pad_rows)
    vt_ref[...] = jnp.concatenate(pieces, axis=0).astype(BF16)


def _inmix(x, mod, mod_row, g1, wh, wr, wt, caw, cab, lag, lab, sgg, sgb, sgw, sgbias, cdw,
           rope_tabs, tm):
    bsz, n_tok, d = x.shape
    nt = n_tok // tm
    hb = tm // HALO
    n_hblk = n_tok // HALO
    cos_n, sina_n, sinb_n, cos_t, sin_t = rope_tabs
    if mod_row is None:
        mod_map = lambda b, i: (b, 0, 0)
    else:
        mod_map = lambda b, i: (mod_row, 0, 0)
    const2 = lambda b, i: (0, 0)
    tok_map = lambda b, i: (i, 0)
    in_specs = [
        pl.BlockSpec((None, tm, d), lambda b, i: (b, i, 0)),
        pl.BlockSpec((None, HALO, d), lambda b, i: (b, jnp.maximum(i * hb - 1, 0), 0)),
        pl.BlockSpec((None, HALO, d), lambda b, i: (b, jnp.minimum((i + 1) * hb, n_hblk - 1), 0)),
        pl.BlockSpec((None, 6, d), mod_map),
        pl.BlockSpec((1, d), const2),
        pl.BlockSpec(wh.shape, const2),
        pl.BlockSpec(wr.shape, const2),
        pl.BlockSpec(wt.shape, const2),
        pl.BlockSpec(caw.shape, const2),
        pl.BlockSpec(cab.shape, const2),
        pl.BlockSpec(lag.shape, const2),
        pl.BlockSpec(lab.shape, const2),
        pl.BlockSpec(sgg.shape, const2),
        pl.BlockSpec(sgb.shape, const2),
        pl.BlockSpec(sgw.shape, lambda b, i: (0, 0, 0)),
        pl.BlockSpec(sgbias.shape, const2),
        pl.BlockSpec(cdw.shape, const2),
        pl.BlockSpec((tm, GROUP_W), tok_map),
        pl.BlockSpec((tm, GROUP_W), tok_map),
        pl.BlockSpec((tm, GROUP_W), tok_map),
        pl.BlockSpec((2 * ROPE_FREQS, tm), lambda b, i: (0, i)),
        pl.BlockSpec((2 * ROPE_FREQS, tm), lambda b, i: (0, i)),
    ]
    out_specs = [
        pl.BlockSpec((None, tm, 3 * GROUP_W), lambda b, i: (b, i, 0)),
        pl.BlockSpec((None, tm, GROUP_W), lambda b, i: (b, i, 0)),
        pl.BlockSpec((None, GROUP_W, tm), lambda b, i: (b, 0, i)),
        pl.BlockSpec((None, DIFF_HEADS * V_ROWS, tm), lambda b, i: (b, 0, i)),
    ]
    out_shape = [
        jax.ShapeDtypeStruct((bsz, n_tok, 3 * GROUP_W), BF16),
        jax.ShapeDtypeStruct((bsz, n_tok, GROUP_W), BF16),
        jax.ShapeDtypeStruct((bsz, GROUP_W, n_tok), BF16),
        jax.ShapeDtypeStruct((bsz, DIFF_HEADS * V_ROWS, n_tok), BF16),
    ]
    yacd, k, qt, vt = pl.pallas_call(
        functools.partial(_inmix_kernel, tm=tm, n_tok=n_tok),
        grid=(bsz, nt),
        in_specs=in_specs,
        out_specs=out_specs,
        out_shape=out_shape,
        scratch_shapes=[pltpu.VMEM((tm + 2 * HALO, GROUP_W), F32),
                        pltpu.VMEM((tm + 2 * HALO, GROUP_W), F32)],
        compiler_params=pltpu.CompilerParams(
            dimension_semantics=("arbitrary", "arbitrary"), vmem_limit_bytes=VMEM_LIMIT),
        name="inmix",
    )(x, x, x, mod, g1, wh, wr, wt, caw, cab, lag, lab, sgg, sgb, sgw, sgbias, cdw,
      cos_n, sina_n, sinb_n, cos_t, sin_t)
    return yacd, k, qt, vt.reshape(bsz, DIFF_HEADS, V_ROWS, n_tok)


def _attn_kernel(lq1_ref, lk1_ref, lq2_ref, lk2_ref, g_ref, qt_ref, k_ref, vt_ref, o_ref,
                 qm_ref, s0_ref, s1_ref, e0_ref, e1_ref, *, tq, tk, n_keys, lam_init):
    hd = pl.program_id(1)
    n_tiles = n_keys // tk
    s_slots, e_slots = (s0_ref, s1_ref), (e0_ref, e1_ref)

    qt = qt_ref[...]
    comp = lax.broadcasted_iota(jnp.int32, (GROUP_W, 1), 0) // DIFF_DQK
    for c in range(2):
        qm_ref[c] = jnp.where(comp == 2 * hd + c, qt, jnp.zeros_like(qt))

    def tile_start(t):
        return t * tk if isinstance(t, int) else pl.multiple_of(t * tk, tk)

    def trip(tau, par, carry):
        m, alpha, acc, mx = carry
        static = isinstance(tau, int)
        do_scores = (not static) or tau < n_tiles
        do_softmax = (not static) or 1 <= tau <= n_tiles
        do_values = (not static) or 2 <= tau <= n_tiles + 1
        new_m, new_alpha, new_acc, new_mx = list(m), list(alpha), list(acc), list(mx)
        if do_values:
            vt = vt_ref[:, pl.ds(tile_start(tau - 2), tk)]
            for c in range(2):
                pv = jnp.dot(vt, e_slots[par][c], preferred_element_type=F32)
                new_acc[c] = alpha[c] * acc[c] + pv
        if do_softmax:
            for c in range(2):
                m_new = jnp.maximum(m[c], mx[c])
                new_alpha[c] = jnp.exp2(m[c] - m_new)
                new_m[c] = m_new
                e_slots[1 - par][c] = jnp.exp2((s_slots[1 - par][c] - m_new).astype(BF16))
        if do_scores:
            kt = k_ref[pl.ds(tile_start(tau), tk), :]
            for c in range(2):
                s = jnp.dot(kt, qm_ref[c], preferred_element_type=F32)
                new_mx[c] = jnp.max(s, axis=0, keepdims=True)
                s_slots[par][c] = s
        return tuple(new_m), tuple(new_alpha), tuple(new_acc), tuple(new_mx)

    row = lambda v: (jnp.full((1, tq), v, F32),) * 2
    carry = (row(NEG_BIG), row(1.0), (jnp.zeros((V_ROWS, tq), F32),) * 2, row(NEG_BIG))
    n_steady = max(n_tiles - 2, 0)
    for tau in range(min(2, n_tiles + 2)):
        carry = trip(tau, tau % 2, carry)

    def pair(j, carry):
        tau = 2 + 2 * j
        return trip(tau + 1, 1, trip(tau, 0, carry))

    carry = lax.fori_loop(0, n_steady // 2, pair, carry)
    for tau in range(2 + 2 * (n_steady // 2), n_tiles + 2):
        carry = trip(tau, tau % 2, carry)

    lam = (jnp.exp(jnp.sum(lq1_ref[...] * lk1_ref[...], axis=-1, keepdims=True))
           - jnp.exp(jnp.sum(lq2_ref[...] * lk2_ref[...], axis=-1, keepdims=True)) + lam_init)
    acc1, acc2 = carry[2]
    o1 = acc1[0:DIFF_DV, :] / acc1[DIFF_DV:DIFF_DV + 1, :]
    o2 = acc2[0:DIFF_DV, :] / acc2[DIFF_DV:DIFF_DV + 1, :]
    o = o1 - lam * o2
    y = o * lax.rsqrt(jnp.mean(o * o, axis=0, keepdims=True) + EPS)
    o_ref[...] = ((y * g_ref[...]) * (1.0 - lam_init)).astype(o_ref.dtype)


def _key_tile(n_keys):
    for tk in (768, 512, 256):
        if n_keys % tk == 0:
            return tk
    raise ValueError(f"key count {n_keys} is not a multiple of 256")


def _attention(lam_params, subln_col, qt, k, vt, lam_init):
    bsz, _, n_q = qt.shape
    n_keys = k.shape[1]
    tq, tk = min(256, n_q), _key_tile(n_keys)
    const2 = lambda b, h, i: (0, 0)
    in_specs = [pl.BlockSpec((1, DIFF_DQK), const2)] * 4 + [
        pl.BlockSpec((DIFF_DV, 1), const2),
        pl.BlockSpec((None, GROUP_W, tq), lambda b, h, i: (b, 0, i)),
        pl.BlockSpec((None, n_keys, GROUP_W), lambda b, h, i: (b, 0, 0)),
        pl.BlockSpec((None, None, V_ROWS, n_keys), lambda b, h, i: (b, h, 0, 0)),
    ]
    return pl.pallas_call(
        functools.partial(_attn_kernel, tq=tq, tk=tk, n_keys=n_keys, lam_init=lam_init),
        grid=(bsz, DIFF_HEADS, n_q // tq),
        in_specs=in_specs,
        out_specs=pl.BlockSpec((None, DIFF_DV, tq), lambda b, h, i: (b, h, i)),
        out_shape=jax.ShapeDtypeStruct((bsz, GROUP_W, n_q), BF16),
        scratch_shapes=[pltpu.VMEM((2, GROUP_W, tq), BF16),
                        pltpu.VMEM((2, tk, tq), F32), pltpu.VMEM((2, tk, tq), F32),
                        pltpu.VMEM((2, tk, tq), BF16), pltpu.VMEM((2, tk, tq), BF16)],
        compiler_params=pltpu.CompilerParams(
            dimension_semantics=("arbitrary", "arbitrary", "arbitrary"), vmem_limit_bytes=VMEM_LIMIT),
        name="diff_attn",
    )(*lam_params, subln_col, qt, k, vt)


def _outmlp_kernel(x_ref, yacd_ref, yt_ref, mod_ref, g2_ref, woacd_ref, wob_ref, w1_ref, w2_ref, fg_ref,
                   o_ref, *, ff_chunk, final_norm):
    gate_mix = mod_ref[2:3, :]
    shift, scale, gate_mlp = mod_ref[3:4, :], mod_ref[4:5, :], mod_ref[5:6, :]
    mix = jnp.dot(yacd_ref[...], woacd_ref[...], preferred_element_type=F32)
    mix = mix + lax.dot_general(yt_ref[...], wob_ref[...], (((0,), (0,)), ((), ())),
                                preferred_element_type=F32)
    x1 = x_ref[...] + gate_mix * mix
    h = _rms_mod(x1, g2_ref[...], shift, scale).astype(BF16)
    d_ff = w1_ref.shape[1]
    mlp = jnp.zeros_like(x1)
    for j in range(d_ff // ff_chunk):
        t = jnp.dot(h, w1_ref[:, j * ff_chunk:(j + 1) * ff_chunk], preferred_element_type=F32)
        t = jnp.square(jnp.maximum(t, 0.0)).astype(BF16)
        mlp = mlp + jnp.dot(t, w2_ref[j * ff_chunk:(j + 1) * ff_chunk, :], preferred_element_type=F32)
    x2 = x1 + gate_mlp * mlp
    if final_norm:
        x2 = x2 * lax.rsqrt(jnp.mean(x2 * x2, axis=-1, keepdims=True) + EPS) * fg_ref[...]
    o_ref[...] = x2


def _outmlp(x, yacd, yt, mod, mod_row, g2, woacd, wob, w1, w2, fg, tm, final_norm):
    bsz, n_tok, d = x.shape
    if mod_row is None:
        mod_map = lambda b, i: (b, 0, 0)
    else:
        mod_map = lambda b, i: (mod_row, 0, 0)
    const2 = lambda b, i: (0, 0)
    return pl.pallas_call(
        functools.partial(_outmlp_kernel, ff_chunk=1024, final_norm=final_norm),
        grid=(bsz, n_tok // tm),
        in_specs=[
            pl.BlockSpec((None, tm, d), lambda b, i: (b, i, 0)),
            pl.BlockSpec((None, tm, 3 * GROUP_W), lambda b, i: (b, i, 0)),
            pl.BlockSpec((None, GROUP_W, tm), lambda b, i: (b, 0, i)),
            pl.BlockSpec((None, 6, d), mod_map),
            pl.BlockSpec((1, d), const2),
            pl.BlockSpec(woacd.shape, const2),
            pl.BlockSpec(wob.shape, const2),
            pl.BlockSpec(w1.shape, const2),
            pl.BlockSpec(w2.shape, const2),
            pl.BlockSpec((1, d), const2),
        ],
        out_specs=pl.BlockSpec((None, tm, d), lambda b, i: (b, i, 0)),
        out_shape=jax.ShapeDtypeStruct((bsz, n_tok, d), F32),
        compiler_params=pltpu.CompilerParams(
            dimension_semantics=("arbitrary", "arbitrary"), vmem_limit_bytes=VMEM_LIMIT),
        name="outmlp",
    )(x, yacd, yt, mod, g2, woacd, wob, w1, w2, fg)


def _rope_tables(n_tok, identity):
    if identity:
        cos_r = cos_c = jnp.ones((n_tok, ROPE_FREQS), F32)
        sin_r = sin_c = jnp.zeros((n_tok, ROPE_FREQS), F32)
    else:
        rows = n_tok // GRID_W
        row = jnp.repeat(jnp.arange(rows, dtype=F32), GRID_W)
        col = jnp.tile(jnp.arange(GRID_W, dtype=F32), rows)
        inv = ROPE_BASE ** (-jnp.arange(0, ROPE_HALF, 2, dtype=F32) / ROPE_HALF)
        ang_r = row[:, None] * inv
        ang_c = col[:, None] * inv
        cos_r, sin_r, cos_c, sin_c = jnp.cos(ang_r), jnp.sin(ang_r), jnp.cos(ang_c), jnp.sin(ang_c)
    zero = jnp.zeros_like(sin_r)
    reps = GROUP_W // DIFF_DQK
    cos_n = jnp.tile(jnp.concatenate([cos_r, cos_r, cos_c, cos_c], axis=1), (1, reps))
    sina_n = jnp.tile(jnp.concatenate([-sin_r, zero, -sin_c, zero], axis=1), (1, reps))
    sinb_n = jnp.tile(jnp.concatenate([zero, sin_r, zero, sin_c], axis=1), (1, reps))
    cos_t = jnp.concatenate([cos_r, cos_c], axis=1).T
    sin_t = jnp.concatenate([sin_r, sin_c], axis=1).T
    return cos_n, sina_n, sinb_n, cos_t, sin_t


def kernel(x, c, ctx, c_ctx, ada_w, ada_b, norm1_g, norm2_g, w_in, conv_a_w, conv_a_b, ln_a_g, ln_a_b,
           lam_q1, lam_k1, lam_q2, lam_k2, subln_g, sg_ln_g, sg_ln_b, sg_w, sg_b, conv_d_w, w_out,
           mlp_w1, mlp_w2, final_g):
    bsz, n_tok, d = x.shape
    n_ctx = ctx.shape[1]
    depth = ada_w.shape[0]
    tm = min(512, n_tok)
    tm_ctx = min(512, n_ctx)
    ctx_row = bsz

    cc = jnp.zeros((8, d), F32).at[:bsz].set(c).at[ctx_row].set(c_ctx)
    mod = _modulation(cc, ada_w, ada_b)

    rope_x = _rope_tables(n_tok, identity=False)
    rope_c = _rope_tables(n_ctx, identity=True)
    row2 = lambda a: a.reshape(1, -1)

    h_ctx = ctx
    for l in range(depth):
        lam_init = 0.8 - 0.6 * math.exp(-0.3 * l)
        parts = [w_in[l][:, p * GROUP_W:(p + 1) * GROUP_W] for p in range(10)]
        wh = jnp.concatenate([parts[0], parts[1], parts[8], parts[9]], axis=1).astype(BF16)
        wr = jnp.concatenate([parts[3], parts[5], parts[6], parts[7]], axis=1).astype(BF16)
        wt = jnp.concatenate([parts[2], parts[4]], axis=1).T.astype(BF16)
        sgbias = jnp.repeat(sg_b[l].T, SG_DIM, axis=1)
        mix_w = (row2(norm1_g[l]), wh, wr, wt, conv_a_w[l], row2(conv_a_b[l]), row2(ln_a_g[l]),
                 row2(ln_a_b[l]), row2(sg_ln_g[l]), row2(sg_ln_b[l]), sg_w[l].astype(BF16), sgbias,
                 conv_d_w[l])
        lam_params = (row2(lam_q1[l]), row2(lam_k1[l]), row2(lam_q2[l]), row2(lam_k2[l]))
        subln_col = subln_g[l].reshape(DIFF_DV, 1)
        woacd = jnp.concatenate([w_out[l][0:GROUP_W], w_out[l][2 * GROUP_W:]], axis=0).astype(BF16)
        wob = w_out[l][GROUP_W:2 * GROUP_W].astype(BF16)
        w1 = mlp_w1[l].astype(BF16)
        w2 = mlp_w2[l].astype(BF16)
        g2 = row2(norm2_g[l])
        fg = row2(final_g)
        last = l == depth - 1

        yacd_c, k_c, qt_c, vt_c = _inmix(h_ctx, mod[l], ctx_row, *mix_w, rope_c, tm_ctx)
        yacd_x, k_x, qt_x, vt_x = _inmix(x, mod[l], None, *mix_w, rope_x, tm)
        k_all = jnp.concatenate([k_x, k_c], axis=1)
        vt_all = jnp.concatenate([vt_x, vt_c], axis=3)
        yt_x = _attention(lam_params, subln_col, qt_x, k_all, vt_all, lam_init)
        x = _outmlp(x, yacd_x, yt_x, mod[l], None, g2, woacd, wob, w1, w2, fg, tm, final_norm=last)
        if not last:
            yt_c = _attention(lam_params, subln_col, qt_c, k_c, vt_c, lam_init)
            h_ctx = _outmlp(h_ctx, yacd_c, yt_c, mod[l], ctx_row, g2, woacd, wob, w1, w2, fg, tm_ctx,
                            final_norm=False)
    return x
```

```python
import functools
import math

import jax
import jax.numpy as jnp
from jax import lax
from jax.experimental import pallas as pl
from jax.experimental.pallas import tpu as pltpu

F32 = jnp.float32
BF16 = jnp.bfloat16

GRID_W = 64
GROUP_W = 256
CONV_A_K = 31
CONV_D_K = 3
DIFF_HEADS = 4
DIFF_DQK = 32
DIFF_DV = 64
ROPE_HALF = DIFF_DQK // 2
ROPE_FREQS = ROPE_HALF // 2
ROPE_BASE = 10000.0
CHUNK = 128
SG_GROUPS = 4
SG_DIM = GROUP_W // SG_GROUPS
EPS = 1e-6

HALO = 16
V_ROWS = 80
TRIPS_PER_ITER = 16
NEG_BIG = -1e30
Q_SCALE = DIFF_DQK ** -0.5 * math.log2(math.e)
VMEM_LIMIT = 56 * 1024 * 1024


def _silu(x):
    return x * jax.nn.sigmoid(x)


def _layer_norm(x, g, b):
    xc = x - jnp.mean(x, axis=-1, keepdims=True)
    y = xc * lax.rsqrt(jnp.mean(xc * xc, axis=-1, keepdims=True) + EPS)
    return y * g + b


def _rms_mod(x, g, shift, scale):
    y = x * lax.rsqrt(jnp.mean(x * x, axis=-1, keepdims=True) + EPS)
    return (y * g) * (1.0 + scale) + shift


def _mod_kernel(cc_ref, w_ref, b_ref, o_ref):
    o_ref[...] = jnp.dot(_silu(cc_ref[...]), w_ref[...], preferred_element_type=F32) + b_ref[...]


def _modulation(cc, ada_w, ada_b):
    depth, d, d6 = ada_w.shape
    out = pl.pallas_call(
        _mod_kernel,
        grid=(depth, d6 // d),
        in_specs=[pl.BlockSpec((8, d), lambda l, j: (0, 0)),
                  pl.BlockSpec((None, d, d), lambda l, j: (l, 0, j)),
                  pl.BlockSpec((None, 1, d), lambda l, j: (l, 0, j))],
        out_specs=pl.BlockSpec((None, 8, d), lambda l, j: (l, 0, j)),
        out_shape=jax.ShapeDtypeStruct((depth, 8, d6), F32),
        name="ada_mod",
    )(cc, ada_w, ada_b.reshape(depth, 1, d6))
    return out.reshape(depth, 8, d6 // d, d)


def _inmix_kernel(x_ref, xp_ref, xn_ref, mod_ref, g1_ref, wh_ref, wr_ref, wt_ref,
                  caw_ref, cab_ref, lag_ref, lab_ref, sgg_ref, sgb_ref, sgw_ref, sgbias_ref, cdw_ref,
                  cos_ref, sina_ref, sinb_ref, cost_ref, sint_ref,
                  yacd_ref, k_ref, qt_ref, vt_ref, glu_scr, p_scr, *, tm, n_tok):
    i = pl.program_id(1)
    shift = mod_ref[0:1, :]
    scale = mod_ref[1:2, :]
    g1 = g1_ref[...]
    h = _rms_mod(x_ref[...], g1, shift, scale).astype(BF16)
    hp = _rms_mod(xp_ref[...], g1, shift, scale).astype(BF16)
    hn = _rms_mod(xn_ref[...], g1, shift, scale).astype(BF16)
    h_ext = jnp.concatenate([hp, h, hn], axis=0)

    zh = jnp.dot(h_ext, wh_ref[...], preferred_element_type=F32)
    zr = jnp.dot(h, wr_ref[...], preferred_element_type=F32)
    zt = lax.dot_general(wt_ref[...], h, (((1,), (1,)), ((), ())),
                         preferred_element_type=F32)

    pos = i * tm - HALO + lax.broadcasted_iota(jnp.int32, (tm + 2 * HALO, 1), 0)
    valid = jnp.logical_and(pos >= 0, pos < n_tok)
    a_val, a_gate = zh[:, 0:GROUP_W], zh[:, GROUP_W:2 * GROUP_W]
    cg, xin = zh[:, 2 * GROUP_W:3 * GROUP_W], zh[:, 3 * GROUP_W:4 * GROUP_W]
    glu_scr[...] = jnp.where(valid, a_val * jax.nn.sigmoid(a_gate), 0.0)
    p_scr[...] = jnp.where(valid, cg * xin, 0.0)

    acc = jnp.zeros((tm, GROUP_W), F32) + cab_ref[...]
    for j in range(CONV_A_K):
        off = HALO - CONV_A_K // 2 + j
        acc = acc + caw_ref[j:j + 1, :] * glu_scr[off:off + tm, :]
    y_a = _silu(_layer_norm(acc, lag_ref[...], lab_ref[...]))

    conv_d = jnp.zeros((tm, GROUP_W), F32)
    for j in range(CONV_D_K):
        off = HALO - CONV_D_K // 2 + j
        conv_d = conv_d + cdw_ref[j:j + 1, :] * p_scr[off:off + tm, :]
    y_d = zr[:, 3 * GROUP_W:4 * GROUP_W] * conv_d

    u = jax.nn.gelu(zr[:, GROUP_W:2 * GROUP_W])
    sv = _layer_norm(jax.nn.gelu(zr[:, 2 * GROUP_W:3 * GROUP_W]), sgg_ref[...], sgb_ref[...]).astype(BF16)
    lane_group = lax.broadcasted_iota(jnp.int32, (CHUNK, GROUP_W), 1) // SG_DIM
    gated = []
    for c in range(tm // CHUNK):
        vc = sv[c * CHUNK:(c + 1) * CHUNK, :]
        s = sgbias_ref[...]
        for g in range(SG_GROUPS):
            sg = jnp.dot(sgw_ref[g], vc, preferred_element_type=F32)
            s = s + jnp.where(lane_group == g, sg, 0.0)
        gated.append(s)
    y_c = u * jnp.concatenate(gated, axis=0)

    yacd_ref[...] = jnp.concatenate([y_a, y_c, y_d], axis=-1).astype(BF16)

    k = zr[:, 0:GROUP_W]
    k = (k * cos_ref[...] + pltpu.roll(k, GROUP_W - ROPE_FREQS, 1) * sina_ref[...]
         + pltpu.roll(k, ROPE_FREQS, 1) * sinb_ref[...])
    k_ref[...] = k.astype(BF16)

    qt = zt[0:GROUP_W, :]
    cos_t, sin_t = cost_ref[...], sint_ref[...]
    rows = []
    for grp in range(GROUP_W // ROPE_HALF):
        lo = qt[grp * ROPE_HALF:grp * ROPE_HALF + ROPE_FREQS, :]
        hi = qt[grp * ROPE_HALF + ROPE_FREQS:(grp + 1) * ROPE_HALF, :]
        axis = grp % 2
        cs = cos_t[axis * ROPE_FREQS:(axis + 1) * ROPE_FREQS, :]
        sn = sin_t[axis * ROPE_FREQS:(axis + 1) * ROPE_FREQS, :]
        rows.append(lo * cs - hi * sn)
        rows.append(lo * sn + hi * cs)
    qt_ref[...] = (jnp.concatenate(rows, axis=0) * Q_SCALE).astype(BF16)

    pad_rows = jnp.where(lax.broadcasted_iota(jnp.int32, (V_ROWS - DIFF_DV, tm), 0) == 0, 1.0, 0.0)
    pieces = []
    for hd in range(DIFF_HEADS):
        pieces.append(zt[GROUP_W + hd * DIFF_DV:GROUP_W + (hd + 1) * DIFF_DV, :])
        pieces.append(pad_rows)
    vt_ref[...] = jnp.concatenate(pieces, axis=0).astype(BF16)


def _inmix(x, mod, mod_row, g1, wh, wr, wt, caw, cab, lag, lab, sgg, sgb, sgw, sgbias, cdw,
           rope_tabs, tm):
    bsz, n_tok, d = x.shape
    nt = n_tok // tm
    hb = tm // HALO
    n_hblk = n_tok // HALO
    cos_n, sina_n, sinb_n, cos_t, sin_t = rope_tabs
    if mod_row is None:
        mod_map = lambda b, i: (b, 0, 0)
    else:
        mod_map = lambda b, i: (mod_row, 0, 0)
    const2 = lambda b, i: (0, 0)
    tok_map = lambda b, i: (i, 0)
    in_specs = [
        pl.BlockSpec((None, tm, d), lambda b, i: (b, i, 0)),
        pl.BlockSpec((None, HALO, d), lambda b, i: (b, jnp.maximum(i * hb - 1, 0), 0)),
        pl.BlockSpec((None, HALO, d), lambda b, i: (b, jnp.minimum((i + 1) * hb, n_hblk - 1), 0)),
        pl.BlockSpec((None, 6, d), mod_map),
        pl.BlockSpec((1, d), const2),
        pl.BlockSpec(wh.shape, const2),
        pl.BlockSpec(wr.shape, const2),
        pl.BlockSpec(wt.shape, const2),
        pl.BlockSpec(caw.shape, const2),
        pl.BlockSpec(cab.shape, const2),
        pl.BlockSpec(lag.shape, const2),
        pl.BlockSpec(lab.shape, const2),
        pl.BlockSpec(sgg.shape, const2),
        pl.BlockSpec(sgb.shape, const2),
        pl.BlockSpec(sgw.shape, lambda b, i: (0, 0, 0)),
        pl.BlockSpec(sgbias.shape, const2),
        pl.BlockSpec(cdw.shape, const2),
        pl.BlockSpec((tm, GROUP_W), tok_map),
        pl.BlockSpec((tm, GROUP_W), tok_map),
        pl.BlockSpec((tm, GROUP_W), tok_map),
        pl.BlockSpec((2 * ROPE_FREQS, tm), lambda b, i: (0, i)),
        pl.BlockSpec((2 * ROPE_FREQS, tm), lambda b, i: (0, i)),
    ]
    out_specs = [
        pl.BlockSpec((None, tm, 3 * GROUP_W), lambda b, i: (b, i, 0)),
        pl.BlockSpec((None, tm, GROUP_W), lambda b, i: (b, i, 0)),
        pl.BlockSpec((None, GROUP_W, tm), lambda b, i: (b, 0, i)),
        pl.BlockSpec((None, DIFF_HEADS * V_ROWS, tm), lambda b, i: (b, 0, i)),
    ]
    out_shape = [
        jax.ShapeDtypeStruct((bsz, n_tok, 3 * GROUP_W), BF16),
        jax.ShapeDtypeStruct((bsz, n_tok, GROUP_W), BF16),
        jax.ShapeDtypeStruct((bsz, GROUP_W, n_tok), BF16),
        jax.ShapeDtypeStruct((bsz, DIFF_HEADS * V_ROWS, n_tok), BF16),
    ]
    yacd, k, qt, vt = pl.pallas_call(
        functools.partial(_inmix_kernel, tm=tm, n_tok=n_tok),
        grid=(bsz, nt),
        in_specs=in_specs,
        out_specs=out_specs,
        out_shape=out_shape,
        scratch_shapes=[pltpu.VMEM((tm + 2 * HALO, GROUP_W), F32),
                        pltpu.VMEM((tm + 2 * HALO, GROUP_W), F32)],
        compiler_params=pltpu.CompilerParams(
            dimension_semantics=("arbitrary", "arbitrary"), vmem_limit_bytes=VMEM_LIMIT),
        name="inmix",
    )(x, x, x, mod, g1, wh, wr, wt, caw, cab, lag, lab, sgg, sgb, sgw, sgbias, cdw,
      cos_n, sina_n, sinb_n, cos_t, sin_t)
    return yacd, k, qt, vt.reshape(bsz, DIFF_HEADS, V_ROWS, n_tok)


def _attn_kernel(lq1_ref, lk1_ref, lq2_ref, lk2_ref, g_ref, qt_ref, k_ref, vt_ref, o_ref,
                 qm_ref, s0_ref, s1_ref, e0_ref, e1_ref, *, tq, tk, n_keys, lam_init):
    hd = pl.program_id(1)
    n_tiles = n_keys // tk
    s_slots, e_slots = (s0_ref, s1_ref), (e0_ref, e1_ref)

    qt = qt_ref[...]
    comp = lax.broadcasted_iota(jnp.int32, (GROUP_W, 1), 0) // DIFF_DQK
    for c in range(2):
        qm_ref[c] = jnp.where(comp == 2 * hd + c, qt, jnp.zeros_like(qt))

    def tile_start(t):
        return t * tk if isinstance(t, int) else pl.multiple_of(t * tk, tk)

    def trip(tau, par, carry):
        m, alpha, acc, mx = carry
        static = isinstance(tau, int)
        do_scores = (not static) or tau < n_tiles
        do_softmax = (not static) or 1 <= tau <= n_tiles
        do_values = (not static) or 2 <= tau <= n_tiles + 1
        new_m, new_alpha, new_acc, new_mx = list(m), list(alpha), list(acc), list(mx)
        if do_values:
            vt = vt_ref[:, pl.ds(tile_start(tau - 2), tk)]
            for c in range(2):
                pv = jnp.dot(vt, e_slots[par][c], preferred_element_type=F32)
                new_acc[c] = alpha[c] * acc[c] + pv
        if do_softmax:
            for c in range(2):
                m_new = jnp.maximum(m[c], mx[c])
                new_alpha[c] = jnp.exp2(m[c] - m_new)
                new_m[c] = m_new
                e_slots[1 - par][c] = jnp.exp2(s_slots[1 - par][c] - m_new.astype(BF16))
        if do_scores:
            kt = k_ref[pl.ds(tile_start(tau), tk), :]
            for c in range(2):
                s = jnp.dot(kt, qm_ref[c], preferred_element_type=F32).astype(BF16)
                new_mx[c] = jnp.max(s, axis=0, keepdims=True).astype(F32)
                s_slots[par][c] = s
        return tuple(new_m), tuple(new_alpha), tuple(new_acc), tuple(new_mx)

    row = lambda v: (jnp.full((1, tq), v, F32),) * 2
    carry = (row(NEG_BIG), row(1.0), (jnp.zeros((V_ROWS, tq), F32),) * 2, row(NEG_BIG))
    n_steady = max(n_tiles - 2, 0)
    for tau in range(min(2, n_tiles + 2)):
        carry = trip(tau, tau % 2, carry)

    def group(j, carry):
        for u in range(TRIPS_PER_ITER):
            carry = trip(2 + TRIPS_PER_ITER * j + u, u % 2, carry)
        return carry

    n_groups = n_steady // TRIPS_PER_ITER
    carry = lax.fori_loop(0, n_groups, group, carry)
    for tau in range(2 + TRIPS_PER_ITER * n_groups, n_tiles + 2):
        carry = trip(tau, tau % 2, carry)

    lam = (jnp.exp(jnp.sum(lq1_ref[...] * lk1_ref[...], axis=-1, keepdims=True))
           - jnp.exp(jnp.sum(lq2_ref[...] * lk2_ref[...], axis=-1, keepdims=True)) + lam_init)
    acc1, acc2 = carry[2]
    o1 = acc1[0:DIFF_DV, :] / acc1[DIFF_DV:DIFF_DV + 1, :]
    o2 = acc2[0:DIFF_DV, :] / acc2[DIFF_DV:DIFF_DV + 1, :]
    o = o1 - lam * o2
    y = o * lax.rsqrt(jnp.mean(o * o, axis=0, keepdims=True) + EPS)
    o_ref[...] = ((y * g_ref[...]) * (1.0 - lam_init)).astype(o_ref.dtype)


def _key_tile(n_keys):
    for tk in (768, 512, 256):
        if n_keys % tk == 0:
            return tk
    raise ValueError(f"key count {n_keys} is not a multiple of 256")


def _attention(lam_params, subln_col, qt, k, vt, lam_init):
    bsz, _, n_q = qt.shape
    n_keys = k.shape[1]
    tq, tk = min(256, n_q), _key_tile(n_keys)
    const2 = lambda b, h, i: (0, 0)
    in_specs = [pl.BlockSpec((1, DIFF_DQK), const2)] * 4 + [
        pl.BlockSpec((DIFF_DV, 1), const2),
        pl.BlockSpec((None, GROUP_W, tq), lambda b, h, i: (b, 0, i)),
        pl.BlockSpec((None, n_keys, GROUP_W), lambda b, h, i: (b, 0, 0)),
        pl.BlockSpec((None, None, V_ROWS, n_keys), lambda b, h, i: (b, h, 0, 0)),
    ]
    return pl.pallas_call(
        functools.partial(_attn_kernel, tq=tq, tk=tk, n_keys=n_keys, lam_init=lam_init),
        grid=(bsz, DIFF_HEADS, n_q // tq),
        in_specs=in_specs,
        out_specs=pl.BlockSpec((None, DIFF_DV, tq), lambda b, h, i: (b, h, i)),
        out_shape=jax.ShapeDtypeStruct((bsz, GROUP_W, n_q), BF16),
        scratch_shapes=[pltpu.VMEM((2, GROUP_W, tq), BF16),
                        pltpu.VMEM((2, tk, tq), BF16), pltpu.VMEM((2, tk, tq), BF16),
                        pltpu.VMEM((2, tk, tq), BF16), pltpu.VMEM((2, tk, tq), BF16)],
        compiler_params=pltpu.CompilerParams(
            dimension_semantics=("arbitrary", "arbitrary", "arbitrary"), vmem_limit_bytes=VMEM_LIMIT),
        name="diff_attn",
    )(*lam_params, subln_col, qt, k, vt)


def _outmlp_kernel(x_ref, yacd_ref, yt_ref, mod_ref, g2_ref, woacd_ref, wob_ref, w1_ref, w2_ref, fg_ref,
                   o_ref, *, ff_chunk, final_norm):
    gate_mix = mod_ref[2:3, :]
    shift, scale, gate_mlp = mod_ref[3:4, :], mod_ref[4:5, :], mod_ref[5:6, :]
    mix = jnp.dot(yacd_ref[...], woacd_ref[...], preferred_element_type=F32)
    mix = mix + lax.dot_general(yt_ref[...], wob_ref[...], (((0,), (0,)), ((), ())),
                                preferred_element_type=F32)
    x1 = x_ref[...] + gate_mix * mix
    h = _rms_mod(x1, g2_ref[...], shift, scale).astype(BF16)
    d_ff = w1_ref.shape[1]
    mlp = jnp.zeros_like(x1)
    for j in range(d_ff // ff_chunk):
        t = jnp.dot(h, w1_ref[:, j * ff_chunk:(j + 1) * ff_chunk], preferred_element_type=F32)
        t = jnp.square(jnp.maximum(t, 0.0)).astype(BF16)
        mlp = mlp + jnp.dot(t, w2_ref[j * ff_chunk:(j + 1) * ff_chunk, :], preferred_element_type=F32)
    x2 = x1 + gate_mlp * mlp
    if final_norm:
        x2 = x2 * lax.rsqrt(jnp.mean(x2 * x2, axis=-1, keepdims=True) + EPS) * fg_ref[...]
    o_ref[...] = x2


def _outmlp(x, yacd, yt, mod, mod_row, g2, woacd, wob, w1, w2, fg, tm, final_norm):
    bsz, n_tok, d = x.shape
    if mod_row is None:
        mod_map = lambda b, i: (b, 0, 0)
    else:
        mod_map = lambda b, i: (mod_row, 0, 0)
    const2 = lambda b, i: (0, 0)
    return pl.pallas_call(
        functools.partial(_outmlp_kernel, ff_chunk=1024, final_norm=final_norm),
        grid=(bsz, n_tok // tm),
        in_specs=[
            pl.BlockSpec((None, tm, d), lambda b, i: (b, i, 0)),
            pl.BlockSpec((None, tm, 3 * GROUP_W), lambda b, i: (b, i, 0)),
            pl.BlockSpec((None, GROUP_W, tm), lambda b, i: (b, 0, i)),
            pl.BlockSpec((None, 6, d), mod_map),
            pl.BlockSpec((1, d), const2),
            pl.BlockSpec(woacd.shape, const2),
            pl.BlockSpec(wob.shape, const2),
            pl.BlockSpec(w1.shape, const2),
            pl.BlockSpec(w2.shape, const2),
            pl.BlockSpec((1, d), const2),
        ],
        out_specs=pl.BlockSpec((None, tm, d), lambda b, i: (b, i, 0)),
        out_shape=jax.ShapeDtypeStruct((bsz, n_tok, d), F32),
        compiler_params=pltpu.CompilerParams(
            dimension_semantics=("arbitrary", "arbitrary"), vmem_limit_bytes=VMEM_LIMIT),
        name="outmlp",
    )(x, yacd, yt, mod, g2, woacd, wob, w1, w2, fg)


def _rope_tables(n_tok, identity):
    if identity:
        cos_r = cos_c = jnp.ones((n_tok, ROPE_FREQS), F32)
        sin_r = sin_c = jnp.zeros((n_tok, ROPE_FREQS), F32)
    else:
        rows = n_tok // GRID_W
        row = jnp.repeat(jnp.arange(rows, dtype=F32), GRID_W)
        col = jnp.tile(jnp.arange(GRID_W, dtype=F32), rows)
        inv = ROPE_BASE ** (-jnp.arange(0, ROPE_HALF, 2, dtype=F32) / ROPE_HALF)
        ang_r = row[:, None] * inv
        ang_c = col[:, None] * inv
        cos_r, sin_r, cos_c, sin_c = jnp.cos(ang_r), jnp.sin(ang_r), jnp.cos(ang_c), jnp.sin(ang_c)
    zero = jnp.zeros_like(sin_r)
    reps = GROUP_W // DIFF_DQK
    cos_n = jnp.tile(jnp.concatenate([cos_r, cos_r, cos_c, cos_c], axis=1), (1, reps))
    sina_n = jnp.tile(jnp.concatenate([-sin_r, zero, -sin_c, zero], axis=1), (1, reps))
    sinb_n = jnp.tile(jnp.concatenate([zero, sin_r, zero, sin_c], axis=1), (1, reps))
    cos_t = jnp.concatenate([cos_r, cos_c], axis=1).T
    sin_t = jnp.concatenate([sin_r, sin_c], axis=1).T
    return cos_n, sina_n, sinb_n, cos_t, sin_t


def kernel(x, c, ctx, c_ctx, ada_w, ada_b, norm1_g, norm2_g, w_in, conv_a_w, conv_a_b, ln_a_g, ln_a_b,
           lam_q1, lam_k1, lam_q2, lam_k2, subln_g, sg_ln_g, sg_ln_b, sg_w, sg_b, conv_d_w, w_out,
           mlp_w1, mlp_w2, final_g):
    bsz, n_tok, d = x.shape
    n_ctx = ctx.shape[1]
    depth = ada_w.shape[0]
    tm = min(512, n_tok)
    tm_ctx = min(512, n_ctx)
    ctx_row = bsz

    cc = jnp.zeros((8, d), F32).at[:bsz].set(c).at[ctx_row].set(c_ctx)
    mod = _modulation(cc, ada_w, ada_b)

    rope_x = _rope_tables(n_tok, identity=False)
    rope_c = _rope_tables(n_ctx, identity=True)
    row2 = lambda a: a.reshape(1, -1)

    h_ctx = ctx
    for l in range(depth):
        lam_init = 0.8 - 0.6 * math.exp(-0.3 * l)
        parts = [w_in[l][:, p * GROUP_W:(p + 1) * GROUP_W] for p in range(10)]
        wh = jnp.concatenate([parts[0], parts[1], parts[8], parts[9]], axis=1).astype(BF16)
        wr = jnp.concatenate([parts[3], parts[5], parts[6], parts[7]], axis=1).astype(BF16)
        wt = jnp.concatenate([parts[2], parts[4]], axis=1).T.astype(BF16)
        sgbias = jnp.repeat(sg_b[l].T, SG_DIM, axis=1)
        mix_w = (row2(norm1_g[l]), wh, wr, wt, conv_a_w[l], row2(conv_a_b[l]), row2(ln_a_g[l]),
                 row2(ln_a_b[l]), row2(sg_ln_g[l]), row2(sg_ln_b[l]), sg_w[l].astype(BF16), sgbias,
                 conv_d_w[l])
        lam_params = (row2(lam_q1[l]), row2(lam_k1[l]), row2(lam_q2[l]), row2(lam_k2[l]))
        subln_col = subln_g[l].reshape(DIFF_DV, 1)
        woacd = jnp.concatenate([w_out[l][0:GROUP_W], w_out[l][2 * GROUP_W:]], axis=0).astype(BF16)
        wob = w_out[l][GROUP_W:2 * GROUP_W].astype(BF16)
        w1 = mlp_w1[l].astype(BF16)
        w2 = mlp_w2[l].astype(BF16)
        g2 = row2(norm2_g[l])
        fg = row2(final_g)
        last = l == depth - 1

        yacd_c, k_c, qt_c, vt_c = _inmix(h_ctx, mod[l], ctx_row, *mix_w, rope_c, tm_ctx)
        yacd_x, k_x, qt_x, vt_x = _inmix(x, mod[l], None, *mix_w, rope_x, tm)
        k_all = jnp.concatenate([k_x, k_c], axis=1)
        vt_all = jnp.concatenate([vt_x, vt_c], axis=3)
        yt_x = _attention(lam_params, subln_col, qt_x, k_all, vt_all, lam_init)
        x = _outmlp(x, yacd_x, yt_x, mod[l], None, g2, woacd, wob, w1, w2, fg, tm, final_norm=last)
        if not last:
            yt_c = _attention(lam_params, subln_col, qt_c, k_c, vt_c, lam_init)
            h_ctx = _outmlp(h_ctx, yacd_c, yt_c, mod[l], ctx_row, g2, woacd, wob, w1, w2, fg, tm_ctx,
                            final_norm=False)
    return x
```

```python
import functools
import math

import jax
import jax.numpy as jnp
from jax import lax
from jax.experimental import pallas as pl
from jax.experimental.pallas import tpu as pltpu

F32 = jnp.float32
BF16 = jnp.bfloat16

GRID_W = 64
GROUP_W = 256
CONV_A_K = 31
CONV_D_K = 3
DIFF_HEADS = 4
DIFF_DQK = 32
DIFF_DV = 64
ROPE_HALF = DIFF_DQK // 2
ROPE_FREQS = ROPE_HALF // 2
ROPE_BASE = 10000.0
CHUNK = 128
SG_GROUPS = 4
SG_DIM = GROUP_W // SG_GROUPS
EPS = 1e-6

SUBLANES = 8
HALO = 16
V_ROWS = 80
TRIPS_PER_ITER = 16
NEG_BIG = -1e30
Q_SCALE = DIFF_DQK ** -0.5 * math.log2(math.e)
VMEM_LIMIT = 56 * 1024 * 1024


def _silu(x):
    return x * jax.nn.sigmoid(x)


def _layer_norm(x, g, b):
    xc = x - jnp.mean(x, axis=-1, keepdims=True)
    y = xc * lax.rsqrt(jnp.mean(xc * xc, axis=-1, keepdims=True) + EPS)
    return y * g + b


def _rms_mod(x, g, shift, scale):
    y = x * lax.rsqrt(jnp.mean(x * x, axis=-1, keepdims=True) + EPS)
    return (y * g) * (1.0 + scale) + shift


def _mod_kernel(cc_ref, w_ref, b_ref, o_ref):
    o_ref[...] = jnp.dot(_silu(cc_ref[...]), w_ref[...], preferred_element_type=F32) + b_ref[...]


def _modulation(cc, ada_w, ada_b):
    depth, d, d6 = ada_w.shape
    out = pl.pallas_call(
        _mod_kernel,
        grid=(depth, d6 // d),
        in_specs=[pl.BlockSpec((8, d), lambda l, j: (0, 0)),
                  pl.BlockSpec((None, d, d), lambda l, j: (l, 0, j)),
                  pl.BlockSpec((None, 1, d), lambda l, j: (l, 0, j))],
        out_specs=pl.BlockSpec((None, 8, d), lambda l, j: (l, 0, j)),
        out_shape=jax.ShapeDtypeStruct((depth, 8, d6), F32),
        name="ada_mod",
    )(cc, ada_w, ada_b.reshape(depth, 1, d6))
    return out.reshape(depth, 8, d6 // d, d)


def _inmix_kernel(x_ref, xp_ref, xn_ref, mod_ref, g1_ref, wh_ref, wr_ref, wt_ref,
                  caw_ref, cab_ref, lag_ref, lab_ref, sgg_ref, sgb_ref, sgw_ref, sgbias_ref, cdw_ref,
                  cos_ref, sina_ref, sinb_ref, cost_ref, sint_ref,
                  yacd_ref, k_ref, qt_ref, vt_ref, *, tm, n_tok):
    i = pl.program_id(1)
    shift = mod_ref[0:1, :]
    scale = mod_ref[1:2, :]
    g1 = g1_ref[...]
    h = _rms_mod(x_ref[...], g1, shift, scale).astype(BF16)
    hp = _rms_mod(xp_ref[...], g1, shift, scale).astype(BF16)
    hn = _rms_mod(xn_ref[...], g1, shift, scale).astype(BF16)
    h_ext = jnp.concatenate([hp, h, hn], axis=0)

    zh = jnp.dot(h_ext, wh_ref[...], preferred_element_type=F32)
    zr = jnp.dot(h, wr_ref[...], preferred_element_type=F32)
    zt = lax.dot_general(wt_ref[...], h, (((1,), (1,)), ((), ())),
                         preferred_element_type=F32)

    pos = i * tm - HALO + lax.broadcasted_iota(jnp.int32, (tm + 2 * HALO, 1), 0)
    valid = jnp.logical_and(pos >= 0, pos < n_tok)
    a_val, a_gate = zh[:, 0:GROUP_W], zh[:, GROUP_W:2 * GROUP_W]
    cg, xin = zh[:, 2 * GROUP_W:3 * GROUP_W], zh[:, 3 * GROUP_W:4 * GROUP_W]
    glu = jnp.where(valid, a_val * jax.nn.sigmoid(a_gate), 0.0)
    p = jnp.where(valid, cg * xin, 0.0)

    def depthwise(ext, w_ref, n_taps):
        rows = ext.shape[0]
        first = HALO - n_taps // 2
        out = jnp.zeros((tm, ext.shape[1]), F32)
        for r in range(SUBLANES):
            taps = [j for j in range(n_taps) if (first + j) % SUBLANES == r]
            if not taps:
                continue
            rolled = ext if r == 0 else pltpu.roll(ext, rows - r, 0)
            for j in taps:
                base = first + j - r
                out = out + w_ref[j:j + 1, :] * rolled[base:base + tm, :]
        return out

    y_a = _silu(_layer_norm(depthwise(glu, caw_ref, CONV_A_K) + cab_ref[...], lag_ref[...], lab_ref[...]))

    y_d = zr[:, 3 * GROUP_W:4 * GROUP_W] * depthwise(p, cdw_ref, CONV_D_K)

    u = jax.nn.gelu(zr[:, GROUP_W:2 * GROUP_W])
    sv = _layer_norm(jax.nn.gelu(zr[:, 2 * GROUP_W:3 * GROUP_W]), sgg_ref[...], sgb_ref[...]).astype(BF16)
    lane_group = lax.broadcasted_iota(jnp.int32, (CHUNK, GROUP_W), 1) // SG_DIM
    gated = []
    for c in range(tm // CHUNK):
        vc = sv[c * CHUNK:(c + 1) * CHUNK, :]
        s = sgbias_ref[...]
        for g in range(SG_GROUPS):
            sg = jnp.dot(sgw_ref[g], vc, preferred_element_type=F32)
            s = s + jnp.where(lane_group == g, sg, 0.0)
        gated.append(s)
    y_c = u * jnp.concatenate(gated, axis=0)

    yacd_ref[...] = jnp.concatenate([y_a, y_c, y_d], axis=-1).astype(BF16)

    k = zr[:, 0:GROUP_W]
    k = (k * cos_ref[...] + pltpu.roll(k, GROUP_W - ROPE_FREQS, 1) * sina_ref[...]
         + pltpu.roll(k, ROPE_FREQS, 1) * sinb_ref[...])
    k_ref[...] = k.astype(BF16)

    qt = zt[0:GROUP_W, :]
    cos_t, sin_t = cost_ref[...], sint_ref[...]
    rows = []
    for grp in range(GROUP_W // ROPE_HALF):
        lo = qt[grp * ROPE_HALF:grp * ROPE_HALF + ROPE_FREQS, :]
        hi = qt[grp * ROPE_HALF + ROPE_FREQS:(grp + 1) * ROPE_HALF, :]
        axis = grp % 2
        cs = cos_t[axis * ROPE_FREQS:(axis + 1) * ROPE_FREQS, :]
        sn = sin_t[axis * ROPE_FREQS:(axis + 1) * ROPE_FREQS, :]
        rows.append(lo * cs - hi * sn)
        rows.append(lo * sn + hi * cs)
    qt_ref[...] = (jnp.concatenate(rows, axis=0) * Q_SCALE).astype(BF16)

    pad_rows = jnp.where(lax.broadcasted_iota(jnp.int32, (V_ROWS - DIFF_DV, tm), 0) == 0, 1.0, 0.0)
    pieces = []
    for hd in range(DIFF_HEADS):
        pieces.append(zt[GROUP_W + hd * DIFF_DV:GROUP_W + (hd + 1) * DIFF_DV, :])
        pieces.append(pad_rows)
    vt_ref[...] = jnp.concatenate(pieces, axis=0).astype(BF16)


def _inmix(x, mod, mod_row, g1, wh, wr, wt, caw, cab, lag, lab, sgg, sgb, sgw, sgbias, cdw,
           rope_tabs, tm):
    bsz, n_tok, d = x.shape
    nt = n_tok // tm
    hb = tm // HALO
    n_hblk = n_tok // HALO
    cos_n, sina_n, sinb_n, cos_t, sin_t = rope_tabs
    if mod_row is None:
        mod_map = lambda b, i: (b, 0, 0)
    else:
        mod_map = lambda b, i: (mod_row, 0, 0)
    const2 = lambda b, i: (0, 0)
    tok_map = lambda b, i: (i, 0)
    in_specs = [
        pl.BlockSpec((None, tm, d), lambda b, i: (b, i, 0)),
        pl.BlockSpec((None, HALO, d), lambda b, i: (b, jnp.maximum(i * hb - 1, 0), 0)),
        pl.BlockSpec((None, HALO, d), lambda b, i: (b, jnp.minimum((i + 1) * hb, n_hblk - 1), 0)),
        pl.BlockSpec((None, 6, d), mod_map),
        pl.BlockSpec((1, d), const2),
        pl.BlockSpec(wh.shape, const2),
        pl.BlockSpec(wr.shape, const2),
        pl.BlockSpec(wt.shape, const2),
        pl.BlockSpec(caw.shape, const2),
        pl.BlockSpec(cab.shape, const2),
        pl.BlockSpec(lag.shape, const2),
        pl.BlockSpec(lab.shape, const2),
        pl.BlockSpec(sgg.shape, const2),
        pl.BlockSpec(sgb.shape, const2),
        pl.BlockSpec(sgw.shape, lambda b, i: (0, 0, 0)),
        pl.BlockSpec(sgbias.shape, const2),
        pl.BlockSpec(cdw.shape, const2),
        pl.BlockSpec((tm, GROUP_W), tok_map),
        pl.BlockSpec((tm, GROUP_W), tok_map),
        pl.BlockSpec((tm, GROUP_W), tok_map),
        pl.BlockSpec((2 * ROPE_FREQS, tm), lambda b, i: (0, i)),
        pl.BlockSpec((2 * ROPE_FREQS, tm), lambda b, i: (0, i)),
    ]
    out_specs = [
        pl.BlockSpec((None, tm, 3 * GROUP_W), lambda b, i: (b, i, 0)),
        pl.BlockSpec((None, tm, GROUP_W), lambda b, i: (b, i, 0)),
        pl.BlockSpec((None, GROUP_W, tm), lambda b, i: (b, 0, i)),
        pl.BlockSpec((None, DIFF_HEADS * V_ROWS, tm), lambda b, i: (b, 0, i)),
    ]
    out_shape = [
        jax.ShapeDtypeStruct((bsz, n_tok, 3 * GROUP_W), BF16),
        jax.ShapeDtypeStruct((bsz, n_tok, GROUP_W), BF16),
        jax.ShapeDtypeStruct((bsz, GROUP_W, n_tok), BF16),
        jax.ShapeDtypeStruct((bsz, DIFF_HEADS * V_ROWS, n_tok), BF16),
    ]
    yacd, k, qt, vt = pl.pallas_call(
        functools.partial(_inmix_kernel, tm=tm, n_tok=n_tok),
        grid=(bsz, nt),
        in_specs=in_specs,
        out_specs=out_specs,
        out_shape=out_shape,
        compiler_params=pltpu.CompilerParams(
            dimension_semantics=("arbitrary", "arbitrary"), vmem_limit_bytes=VMEM_LIMIT),
        name="inmix",
    )(x, x, x, mod, g1, wh, wr, wt, caw, cab, lag, lab, sgg, sgb, sgw, sgbias, cdw,
      cos_n, sina_n, sinb_n, cos_t, sin_t)
    return yacd, k, qt, vt.reshape(bsz, DIFF_HEADS, V_ROWS, n_tok)


def _attn_kernel(lq1_ref, lk1_ref, lq2_ref, lk2_ref, g_ref, qt_ref, k_ref, vt_ref, o_ref,
                 qm_ref, s0_ref, s1_ref, e0_ref, e1_ref, *, tq, tk, n_keys, lam_init):
    hd = pl.program_id(1)
    n_tiles = n_keys // tk
    s_slots, e_slots = (s0_ref, s1_ref), (e0_ref, e1_ref)

    qt = qt_ref[...]
    comp = lax.broadcasted_iota(jnp.int32, (GROUP_W, 1), 0) // DIFF_DQK
    for c in range(2):
        qm_ref[c] = jnp.where(comp == 2 * hd + c, qt, jnp.zeros_like(qt))

    def tile_start(t):
        return t * tk if isinstance(t, int) else pl.multiple_of(t * tk, tk)

    def trip(tau, par, carry):
        m, alpha, acc, mx = carry
        static = isinstance(tau, int)
        do_scores = (not static) or tau < n_tiles
        do_softmax = (not static) or 1 <= tau <= n_tiles
        do_values = (not static) or 2 <= tau <= n_tiles + 1
        new_m, new_alpha, new_acc, new_mx = list(m), list(alpha), list(acc), list(mx)
        if do_values:
            vt = vt_ref[:, pl.ds(tile_start(tau - 2), tk)]
            for c in range(2):
                pv = jnp.dot(vt, e_slots[par][c], preferred_element_type=F32)
                new_acc[c] = alpha[c] * acc[c] + pv
        if do_softmax:
            for c in range(2):
                m_new = jnp.maximum(m[c], mx[c])
                new_alpha[c] = jnp.exp2(m[c] - m_new)
                new_m[c] = m_new
                d = s_slots[1 - par][c] - m_new.astype(BF16)
                e_slots[1 - par][c] = jnp.exp2(d.astype(F32)).astype(BF16)
        if do_scores:
            kt = k_ref[pl.ds(tile_start(tau), tk), :]
            for c in range(2):
                s = jnp.dot(kt, qm_ref[c], preferred_element_type=F32).astype(BF16)
                new_mx[c] = jnp.max(s, axis=0, keepdims=True).astype(F32)
                s_slots[par][c] = s
        return tuple(new_m), tuple(new_alpha), tuple(new_acc), tuple(new_mx)

    row = lambda v: (jnp.full((1, tq), v, F32),) * 2
    carry = (row(NEG_BIG), row(1.0), (jnp.zeros((V_ROWS, tq), F32),) * 2, row(NEG_BIG))
    n_steady = max(n_tiles - 2, 0)
    for tau in range(min(2, n_tiles + 2)):
        carry = trip(tau, tau % 2, carry)

    def group(j, carry):
        for u in range(TRIPS_PER_ITER):
            carry = trip(2 + TRIPS_PER_ITER * j + u, u % 2, carry)
        return carry

    n_groups = n_steady // TRIPS_PER_ITER
    carry = lax.fori_loop(0, n_groups, group, carry)
    for tau in range(2 + TRIPS_PER_ITER * n_groups, n_tiles + 2):
        carry = trip(tau, tau % 2, carry)

    lam = (jnp.exp(jnp.sum(lq1_ref[...] * lk1_ref[...], axis=-1, keepdims=True))
           - jnp.exp(jnp.sum(lq2_ref[...] * lk2_ref[...], axis=-1, keepdims=True)) + lam_init)
    acc1, acc2 = carry[2]
    o1 = acc1[0:DIFF_DV, :] / acc1[DIFF_DV:DIFF_DV + 1, :]
    o2 = acc2[0:DIFF_DV, :] / acc2[DIFF_DV:DIFF_DV + 1, :]
    o = o1 - lam * o2
    y = o * lax.rsqrt(jnp.mean(o * o, axis=0, keepdims=True) + EPS)
    o_ref[...] = ((y * g_ref[...]) * (1.0 - lam_init)).astype(o_ref.dtype)


def _key_tile(n_keys):
    for tk in (768, 512, 256):
        if n_keys % tk == 0:
            return tk
    raise ValueError(f"key count {n_keys} is not a multiple of 256")


def _attention(lam_params, subln_col, qt, k, vt, lam_init):
    bsz, _, n_q = qt.shape
    n_keys = k.shape[1]
    tq, tk = min(256, n_q), _key_tile(n_keys)
    const2 = lambda b, h, i: (0, 0)
    in_specs = [pl.BlockSpec((1, DIFF_DQK), const2)] * 4 + [
        pl.BlockSpec((DIFF_DV, 1), const2),
        pl.BlockSpec((None, GROUP_W, tq), lambda b, h, i: (b, 0, i)),
        pl.BlockSpec((None, n_keys, GROUP_W), lambda b, h, i: (b, 0, 0)),
        pl.BlockSpec((None, None, V_ROWS, n_keys), lambda b, h, i: (b, h, 0, 0)),
    ]
    return pl.pallas_call(
        functools.partial(_attn_kernel, tq=tq, tk=tk, n_keys=n_keys, lam_init=lam_init),
        grid=(bsz, DIFF_HEADS, n_q // tq),
        in_specs=in_specs,
        out_specs=pl.BlockSpec((None, DIFF_DV, tq), lambda b, h, i: (b, h, i)),
        out_shape=jax.ShapeDtypeStruct((bsz, GROUP_W, n_q), BF16),
        scratch_shapes=[pltpu.VMEM((2, GROUP_W, tq), BF16),
                        pltpu.VMEM((2, tk, tq), BF16), pltpu.VMEM((2, tk, tq), BF16),
                        pltpu.VMEM((2, tk, tq), BF16), pltpu.VMEM((2, tk, tq), BF16)],
        compiler_params=pltpu.CompilerParams(
            dimension_semantics=("arbitrary", "arbitrary", "arbitrary"), vmem_limit_bytes=VMEM_LIMIT),
        name="diff_attn",
    )(*lam_params, subln_col, qt, k, vt)


def _outmlp_kernel(x_ref, yacd_ref, yt_ref, mod_ref, g2_ref, woacd_ref, wob_ref, w1_ref, w2_ref, fg_ref,
                   o_ref, *, ff_chunk, final_norm):
    gate_mix = mod_ref[2:3, :]
    shift, scale, gate_mlp = mod_ref[3:4, :], mod_ref[4:5, :], mod_ref[5:6, :]
    mix = jnp.dot(yacd_ref[...], woacd_ref[...], preferred_element_type=F32)
    mix = mix + lax.dot_general(yt_ref[...], wob_ref[...], (((0,), (0,)), ((), ())),
                                preferred_element_type=F32)
    x1 = x_ref[...] + gate_mix * mix
    h = _rms_mod(x1, g2_ref[...], shift, scale).astype(BF16)
    d_ff = w1_ref.shape[1]
    mlp = jnp.zeros_like(x1)
    for j in range(d_ff // ff_chunk):
        t = jnp.dot(h, w1_ref[:, j * ff_chunk:(j + 1) * ff_chunk], preferred_element_type=F32)
        t = jnp.square(jnp.maximum(t, 0.0)).astype(BF16)
        mlp = mlp + jnp.dot(t, w2_ref[j * ff_chunk:(j + 1) * ff_chunk, :], preferred_element_type=F32)
    x2 = x1 + gate_mlp * mlp
    if final_norm:
        x2 = x2 * lax.rsqrt(jnp.mean(x2 * x2, axis=-1, keepdims=True) + EPS) * fg_ref[...]
    o_ref[...] = x2


def _outmlp(x, yacd, yt, mod, mod_row, g2, woacd, wob, w1, w2, fg, tm, final_norm):
    bsz, n_tok, d = x.shape
    if mod_row is None:
        mod_map = lambda b, i: (b, 0, 0)
    else:
        mod_map = lambda b, i: (mod_row, 0, 0)
    const2 = lambda b, i: (0, 0)
    return pl.pallas_call(
        functools.partial(_outmlp_kernel, ff_chunk=1024, final_norm=final_norm),
        grid=(bsz, n_tok // tm),
        in_specs=[
            pl.BlockSpec((None, tm, d), lambda b, i: (b, i, 0)),
            pl.BlockSpec((None, tm, 3 * GROUP_W), lambda b, i: (b, i, 0)),
            pl.BlockSpec((None, GROUP_W, tm), lambda b, i: (b, 0, i)),
            pl.BlockSpec((None, 6, d), mod_map),
            pl.BlockSpec((1, d), const2),
            pl.BlockSpec(woacd.shape, const2),
            pl.BlockSpec(wob.shape, const2),
            pl.BlockSpec(w1.shape, const2),
            pl.BlockSpec(w2.shape, const2),
            pl.BlockSpec((1, d), const2),
        ],
        out_specs=pl.BlockSpec((None, tm, d), lambda b, i: (b, i, 0)),
        out_shape=jax.ShapeDtypeStruct((bsz, n_tok, d), F32),
        compiler_params=pltpu.CompilerParams(
            dimension_semantics=("arbitrary", "arbitrary"), vmem_limit_bytes=VMEM_LIMIT),
        name="outmlp",
    )(x, yacd, yt, mod, g2, woacd, wob, w1, w2, fg)


def _rope_tables(n_tok, identity):
    if identity:
        cos_r = cos_c = jnp.ones((n_tok, ROPE_FREQS), F32)
        sin_r = sin_c = jnp.zeros((n_tok, ROPE_FREQS), F32)
    else:
        rows = n_tok // GRID_W
        row = jnp.repeat(jnp.arange(rows, dtype=F32), GRID_W)
        col = jnp.tile(jnp.arange(GRID_W, dtype=F32), rows)
        inv = ROPE_BASE ** (-jnp.arange(0, ROPE_HALF, 2, dtype=F32) / ROPE_HALF)
        ang_r = row[:, None] * inv
        ang_c = col[:, None] * inv
        cos_r, sin_r, cos_c, sin_c = jnp.cos(ang_r), jnp.sin(ang_r), jnp.cos(ang_c), jnp.sin(ang_c)
    zero = jnp.zeros_like(sin_r)
    reps = GROUP_W // DIFF_DQK
    cos_n = jnp.tile(jnp.concatenate([cos_r, cos_r, cos_c, cos_c], axis=1), (1, reps))
    sina_n = jnp.tile(jnp.concatenate([-sin_r, zero, -sin_c, zero], axis=1), (1, reps))
    sinb_n = jnp.tile(jnp.concatenate([zero, sin_r, zero, sin_c], axis=1), (1, reps))
    cos_t = jnp.concatenate([cos_r, cos_c], axis=1).T
    sin_t = jnp.concatenate([sin_r, sin_c], axis=1).T
    return cos_n, sina_n, sinb_n, cos_t, sin_t


def kernel(x, c, ctx, c_ctx, ada_w, ada_b, norm1_g, norm2_g, w_in, conv_a_w, conv_a_b, ln_a_g, ln_a_b,
           lam_q1, lam_k1, lam_q2, lam_k2, subln_g, sg_ln_g, sg_ln_b, sg_w, sg_b, conv_d_w, w_out,
           mlp_w1, mlp_w2, final_g):
    bsz, n_tok, d = x.shape
    n_ctx = ctx.shape[1]
    depth = ada_w.shape[0]
    tm = min(512, n_tok)
    tm_ctx = min(512, n_ctx)
    ctx_row = bsz

    cc = jnp.zeros((8, d), F32).at[:bsz].set(c).at[ctx_row].set(c_ctx)
    mod = _modulation(cc, ada_w, ada_b)

    rope_x = _rope_tables(n_tok, identity=False)
    rope_c = _rope_tables(n_ctx, identity=True)
    row2 = lambda a: a.reshape(1, -1)

    h_ctx = ctx
    for l in range(depth):
        lam_init = 0.8 - 0.6 * math.exp(-0.3 * l)
        parts = [w_in[l][:, p * GROUP_W:(p + 1) * GROUP_W] for p in range(10)]
        wh = jnp.concatenate([parts[0], parts[1], parts[8], parts[9]], axis=1).astype(BF16)
        wr = jnp.concatenate([parts[3], parts[5], parts[6], parts[7]], axis=1).astype(BF16)
        wt = jnp.concatenate([parts[2], parts[4]], axis=1).T.astype(BF16)
        sgbias = jnp.repeat(sg_b[l].T, SG_DIM, axis=1)
        mix_w = (row2(norm1_g[l]), wh, wr, wt, conv_a_w[l], row2(conv_a_b[l]), row2(ln_a_g[l]),
                 row2(ln_a_b[l]), row2(sg_ln_g[l]), row2(sg_ln_b[l]), sg_w[l].astype(BF16), sgbias,
                 conv_d_w[l])
        lam_params = (row2(lam_q1[l]), row2(lam_k1[l]), row2(lam_q2[l]), row2(lam_k2[l]))
        subln_col = subln_g[l].reshape(DIFF_DV, 1)
        woacd = jnp.concatenate([w_out[l][0:GROUP_W], w_out[l][2 * GROUP_W:]], axis=0).astype(BF16)
        wob = w_out[l][GROUP_W:2 * GROUP_W].astype(BF16)
        w1 = mlp_w1[l].astype(BF16)
        w2 = mlp_w2[l].astype(BF16)
        g2 = row2(norm2_g[l])
        fg = row2(final_g)
        last = l == depth - 1

        yacd_c, k_c, qt_c, vt_c = _inmix(h_ctx, mod[l], ctx_row, *mix_w, rope_c, tm_ctx)
        yacd_x, k_x, qt_x, vt_x = _inmix(x, mod[l], None, *mix_w, rope_x, tm)
        k_all = jnp.concatenate([k_x, k_c], axis=1)
        vt_all = jnp.concatenate([vt_x, vt_c], axis=3)
        yt_x = _attention(lam_params, subln_col, qt_x, k_all, vt_all, lam_init)
        x = _outmlp(x, yacd_x, yt_x, mod[l], None, g2, woacd, wob, w1, w2, fg, tm, final_norm=last)
        if not last:
            yt_c = _attention(lam_params, subln_col, qt_c, k_c, vt_c, lam_init)
            h_ctx = _outmlp(h_ctx, yacd_c, yt_c, mod[l], ctx_row, g2, woacd, wob, w1, w2, fg, tm_ctx,
                            final_norm=False)
    return x
```

```python
import functools
import math

import jax
import jax.numpy as jnp
from jax import lax
from jax.experimental import pallas as pl
from jax.experimental.pallas import tpu as pltpu

F32 = jnp.float32
BF16 = jnp.bfloat16

GRID_W = 64
GROUP_W = 256
CONV_A_K = 31
CONV_D_K = 3
DIFF_HEADS = 4
DIFF_DQK = 32
DIFF_DV = 64
ROPE_HALF = DIFF_DQK // 2
ROPE_FREQS = ROPE_HALF // 2
ROPE_BASE = 10000.0
CHUNK = 128
SG_GROUPS = 4
SG_DIM = GROUP_W // SG_GROUPS
EPS = 1e-6

SUBLANES = 8
HALO = 16
V_ROWS = 80
TRIPS_PER_ITER = 16
NEG_BIG = -1e30
Q_SCALE = DIFF_DQK ** -0.5 * math.log2(math.e)
VMEM_LIMIT = 56 * 1024 * 1024


def _silu(x):
    return x * jax.nn.sigmoid(x)


def _layer_norm(x, g, b):
    xc = x - jnp.mean(x, axis=-1, keepdims=True)
    y = xc * lax.rsqrt(jnp.mean(xc * xc, axis=-1, keepdims=True) + EPS)
    return y * g + b


def _rms_mod(x, g, shift, scale):
    y = x * lax.rsqrt(jnp.mean(x * x, axis=-1, keepdims=True) + EPS)
    return (y * g) * (1.0 + scale) + shift


def _mod_kernel(cc_ref, w_ref, b_ref, o_ref):
    o_ref[...] = jnp.dot(_silu(cc_ref[...]), w_ref[...], preferred_element_type=F32) + b_ref[...]


def _modulation(cc, ada_w, ada_b):
    depth, d, d6 = ada_w.shape
    out = pl.pallas_call(
        _mod_kernel,
        grid=(depth, d6 // d),
        in_specs=[pl.BlockSpec((8, d), lambda l, j: (0, 0)),
                  pl.BlockSpec((None, d, d), lambda l, j: (l, 0, j)),
                  pl.BlockSpec((None, 1, d), lambda l, j: (l, 0, j))],
        out_specs=pl.BlockSpec((None, 8, d), lambda l, j: (l, 0, j)),
        out_shape=jax.ShapeDtypeStruct((depth, 8, d6), F32),
        name="ada_mod",
    )(cc, ada_w, ada_b.reshape(depth, 1, d6))
    return out.reshape(depth, 8, d6 // d, d)


def _inmix_kernel(x_ref, xp_ref, xn_ref, mod_ref, g1_ref, wh_ref, wr_ref, wt_ref,
                  caw_ref, cab_ref, lag_ref, lab_ref, sgg_ref, sgb_ref, sgw_ref, sgbias_ref, cdw_ref,
                  cos_ref, sina_ref, sinb_ref, cost_ref, sint_ref,
                  yacd_ref, k_ref, qt_ref, vt_ref, *, tm, n_tok):
    i = pl.program_id(1)
    shift = mod_ref[0:1, :]
    scale = mod_ref[1:2, :]
    g1 = g1_ref[...]
    h = _rms_mod(x_ref[...], g1, shift, scale).astype(BF16)
    hp = _rms_mod(xp_ref[...], g1, shift, scale).astype(BF16)
    hn = _rms_mod(xn_ref[...], g1, shift, scale).astype(BF16)
    h_ext = jnp.concatenate([hp, h, hn], axis=0)

    zh = jnp.dot(h_ext, wh_ref[...], preferred_element_type=F32)
    zr = jnp.dot(h, wr_ref[...], preferred_element_type=F32)
    zt = lax.dot_general(wt_ref[...], h, (((1,), (1,)), ((), ())),
                         preferred_element_type=F32)

    pos = i * tm - HALO + lax.broadcasted_iota(jnp.int32, (tm + 2 * HALO, 1), 0)
    valid = jnp.logical_and(pos >= 0, pos < n_tok)
    a_val, a_gate = zh[:, 0:GROUP_W], zh[:, GROUP_W:2 * GROUP_W]
    cg, xin = zh[:, 2 * GROUP_W:3 * GROUP_W], zh[:, 3 * GROUP_W:4 * GROUP_W]
    glu = jnp.where(valid, a_val * jax.nn.sigmoid(a_gate), 0.0)
    p = jnp.where(valid, cg * xin, 0.0)

    def depthwise(ext, w_ref, n_taps):
        rows = ext.shape[0]
        first = HALO - n_taps // 2
        out = jnp.zeros((tm, ext.shape[1]), F32)
        for r in range(SUBLANES):
            taps = [j for j in range(n_taps) if (first + j) % SUBLANES == r]
            if not taps:
                continue
            rolled = ext if r == 0 else pltpu.roll(ext, rows - r, 0)
            for j in taps:
                base = first + j - r
                out = out + w_ref[j:j + 1, :] * rolled[base:base + tm, :]
        return out

    y_a = _silu(_layer_norm(depthwise(glu, caw_ref, CONV_A_K) + cab_ref[...], lag_ref[...], lab_ref[...]))

    y_d = zr[:, 3 * GROUP_W:4 * GROUP_W] * depthwise(p, cdw_ref, CONV_D_K)

    u = jax.nn.gelu(zr[:, GROUP_W:2 * GROUP_W])
    sv = _layer_norm(jax.nn.gelu(zr[:, 2 * GROUP_W:3 * GROUP_W]), sgg_ref[...], sgb_ref[...]).astype(BF16)
    lane_group = lax.broadcasted_iota(jnp.int32, (CHUNK, GROUP_W), 1) // SG_DIM
    gated = []
    for c in range(tm // CHUNK):
        vc = sv[c * CHUNK:(c + 1) * CHUNK, :]
        s = sgbias_ref[...]
        for g in range(SG_GROUPS):
            sg = jnp.dot(sgw_ref[g], vc, preferred_element_type=F32)
            s = s + jnp.where(lane_group == g, sg, 0.0)
        gated.append(s)
    y_c = u * jnp.concatenate(gated, axis=0)

    yacd_ref[...] = jnp.concatenate([y_a, y_c, y_d], axis=-1).astype(BF16)

    k = zr[:, 0:GROUP_W]
    k = (k * cos_ref[...] + pltpu.roll(k, GROUP_W - ROPE_FREQS, 1) * sina_ref[...]
         + pltpu.roll(k, ROPE_FREQS, 1) * sinb_ref[...])
    k_ref[...] = k.astype(BF16)

    qt = zt[0:GROUP_W, :]
    cos_t, sin_t = cost_ref[...], sint_ref[...]
    rows = []
    for grp in range(GROUP_W // ROPE_HALF):
        lo = qt[grp * ROPE_HALF:grp * ROPE_HALF + ROPE_FREQS, :]
        hi = qt[grp * ROPE_HALF + ROPE_FREQS:(grp + 1) * ROPE_HALF, :]
        axis = grp % 2
        cs = cos_t[axis * ROPE_FREQS:(axis + 1) * ROPE_FREQS, :]
        sn = sin_t[axis * ROPE_FREQS:(axis + 1) * ROPE_FREQS, :]
        rows.append(lo * cs - hi * sn)
        rows.append(lo * sn + hi * cs)
    qt_ref[...] = (jnp.concatenate(rows, axis=0) * Q_SCALE).astype(BF16)

    pad_rows = jnp.where(lax.broadcasted_iota(jnp.int32, (V_ROWS - DIFF_DV, tm), 0) == 0, 1.0, 0.0)
    pieces = []
    for hd in range(DIFF_HEADS):
        pieces.append(zt[GROUP_W + hd * DIFF_DV:GROUP_W + (hd + 1) * DIFF_DV, :])
        pieces.append(pad_rows)
    vt_ref[...] = jnp.concatenate(pieces, axis=0).astype(BF16)


def _inmix(x, mod, mod_row, g1, wh, wr, wt, caw, cab, lag, lab, sgg, sgb, sgw, sgbias, cdw,
           rope_tabs, tm):
    bsz, n_tok, d = x.shape
    nt = n_tok // tm
    hb = tm // HALO
    n_hblk = n_tok // HALO
    cos_n, sina_n, sinb_n, cos_t, sin_t = rope_tabs
    if mod_row is None:
        mod_map = lambda b, i: (b, 0, 0)
    else:
        mod_map = lambda b, i: (mod_row, 0, 0)
    const2 = lambda b, i: (0, 0)
    tok_map = lambda b, i: (i, 0)
    in_specs = [
        pl.BlockSpec((None, tm, d), lambda b, i: (b, i, 0)),
        pl.BlockSpec((None, HALO, d), lambda b, i: (b, jnp.maximum(i * hb - 1, 0), 0)),
        pl.BlockSpec((None, HALO, d), lambda b, i: (b, jnp.minimum((i + 1) * hb, n_hblk - 1), 0)),
        pl.BlockSpec((None, 6, d), mod_map),
        pl.BlockSpec((1, d), const2),
        pl.BlockSpec(wh.shape, const2),
        pl.BlockSpec(wr.shape, const2),
        pl.BlockSpec(wt.shape, const2),
        pl.BlockSpec(caw.shape, const2),
        pl.BlockSpec(cab.shape, const2),
        pl.BlockSpec(lag.shape, const2),
        pl.BlockSpec(lab.shape, const2),
        pl.BlockSpec(sgg.shape, const2),
        pl.BlockSpec(sgb.shape, const2),
        pl.BlockSpec(sgw.shape, lambda b, i: (0, 0, 0)),
        pl.BlockSpec(sgbias.shape, const2),
        pl.BlockSpec(cdw.shape, const2),
        pl.BlockSpec((tm, GROUP_W), tok_map),
        pl.BlockSpec((tm, GROUP_W), tok_map),
        pl.BlockSpec((tm, GROUP_W), tok_map),
        pl.BlockSpec((2 * ROPE_FREQS, tm), lambda b, i: (0, i)),
        pl.BlockSpec((2 * ROPE_FREQS, tm), lambda b, i: (0, i)),
    ]
    out_specs = [
        pl.BlockSpec((None, tm, 3 * GROUP_W), lambda b, i: (b, i, 0)),
        pl.BlockSpec((None, tm, GROUP_W), lambda b, i: (b, i, 0)),
        pl.BlockSpec((None, GROUP_W, tm), lambda b, i: (b, 0, i)),
        pl.BlockSpec((None, DIFF_HEADS * V_ROWS, tm), lambda b, i: (b, 0, i)),
    ]
    out_shape = [
        jax.ShapeDtypeStruct((bsz, n_tok, 3 * GROUP_W), BF16),
        jax.ShapeDtypeStruct((bsz, n_tok, GROUP_W), BF16),
        jax.ShapeDtypeStruct((bsz, GROUP_W, n_tok), BF16),
        jax.ShapeDtypeStruct((bsz, DIFF_HEADS * V_ROWS, n_tok), BF16),
    ]
    yacd, k, qt, vt = pl.pallas_call(
        functools.partial(_inmix_kernel, tm=tm, n_tok=n_tok),
        grid=(bsz, nt),
        in_specs=in_specs,
        out_specs=out_specs,
        out_shape=out_shape,
        compiler_params=pltpu.CompilerParams(
            dimension_semantics=("arbitrary", "arbitrary"), vmem_limit_bytes=VMEM_LIMIT),
        name="inmix",
    )(x, x, x, mod, g1, wh, wr, wt, caw, cab, lag, lab, sgg, sgb, sgw, sgbias, cdw,
      cos_n, sina_n, sinb_n, cos_t, sin_t)
    return yacd, k, qt, vt.reshape(bsz, DIFF_HEADS, V_ROWS, n_tok)


def _attn_kernel(lq1_ref, lk1_ref, lq2_ref, lk2_ref, g_ref, qt_ref, k_ref, vt_ref, o_ref,
                 qm_ref, s0_ref, s1_ref, e0_ref, e1_ref, *, tq, tk, n_keys, lam_init):
    hd = pl.program_id(1)
    n_tiles = n_keys // tk
    s_slots, e_slots = (s0_ref, s1_ref), (e0_ref, e1_ref)

    qt = qt_ref[...]
    comp = lax.broadcasted_iota(jnp.int32, (GROUP_W, 1), 0) // DIFF_DQK
    for c in range(2):
        qm_ref[c] = jnp.where(comp == 2 * hd + c, qt, jnp.zeros_like(qt))

    def tile_start(t):
        return t * tk if isinstance(t, int) else pl.multiple_of(t * tk, tk)

    def trip(tau, par, carry):
        m, alpha, acc, mx = carry
        static = isinstance(tau, int)
        do_scores = (not static) or tau < n_tiles
        do_softmax = (not static) or 1 <= tau <= n_tiles
        do_values = (not static) or 2 <= tau <= n_tiles + 1
        new_m, new_alpha, new_acc, new_mx = list(m), list(alpha), list(acc), list(mx)
        if do_values:
            vt = vt_ref[:, pl.ds(tile_start(tau - 2), tk)]
            for c in range(2):
                pv = jnp.dot(vt, e_slots[par][c], preferred_element_type=F32)
                new_acc[c] = alpha[c] * acc[c] + pv
        if do_softmax:
            for c in range(2):
                m_new = jnp.maximum(m[c], mx[c])
                new_alpha[c] = jnp.exp2(m[c] - m_new)
                new_m[c] = m_new
                e_slots[1 - par][c] = jnp.exp2(s_slots[1 - par][c] - m_new.astype(BF16))
        if do_scores:
            kt = k_ref[pl.ds(tile_start(tau), tk), :]
            for c in range(2):
                s = jnp.dot(kt, qm_ref[c], preferred_element_type=F32).astype(BF16)
                new_mx[c] = jnp.max(s, axis=0, keepdims=True).astype(F32)
                s_slots[par][c] = s
        return tuple(new_m), tuple(new_alpha), tuple(new_acc), tuple(new_mx)

    row = lambda v: (jnp.full((1, tq), v, F32),) * 2
    carry = (row(NEG_BIG), row(1.0), (jnp.zeros((V_ROWS, tq), F32),) * 2, row(NEG_BIG))
    n_steady = max(n_tiles - 2, 0)
    for tau in range(min(2, n_tiles + 2)):
        carry = trip(tau, tau % 2, carry)

    def group(j, carry):
        for u in range(TRIPS_PER_ITER):
            carry = trip(2 + TRIPS_PER_ITER * j + u, u % 2, carry)
        return carry

    n_groups = n_steady // TRIPS_PER_ITER
    carry = lax.fori_loop(0, n_groups, group, carry)
    for tau in range(2 + TRIPS_PER_ITER * n_groups, n_tiles + 2):
        carry = trip(tau, tau % 2, carry)

    lam = (jnp.exp(jnp.sum(lq1_ref[...] * lk1_ref[...], axis=-1, keepdims=True))
           - jnp.exp(jnp.sum(lq2_ref[...] * lk2_ref[...], axis=-1, keepdims=True)) + lam_init)
    acc1, acc2 = carry[2]
    o1 = acc1[0:DIFF_DV, :] / acc1[DIFF_DV:DIFF_DV + 1, :]
    o2 = acc2[0:DIFF_DV, :] / acc2[DIFF_DV:DIFF_DV + 1, :]
    o = o1 - lam * o2
    y = o * lax.rsqrt(jnp.mean(o * o, axis=0, keepdims=True) + EPS)
    o_ref[...] = ((y * g_ref[...]) * (1.0 - lam_init)).astype(o_ref.dtype)


def _key_tile(n_keys):
    for tk in (768, 512, 256):
        if n_keys % tk == 0:
            return tk
    raise ValueError(f"key count {n_keys} is not a multiple of 256")


def _attention(lam_params, subln_col, qt, k, vt, lam_init):
    bsz, _, n_q = qt.shape
    n_keys = k.shape[1]
    tq, tk = min(512, n_q), _key_tile(n_keys)
    const2 = lambda b, h, i: (0, 0)
    in_specs = [pl.BlockSpec((1, DIFF_DQK), const2)] * 4 + [
        pl.BlockSpec((DIFF_DV, 1), const2),
        pl.BlockSpec((None, GROUP_W, tq), lambda b, h, i: (b, 0, i)),
        pl.BlockSpec((None, n_keys, GROUP_W), lambda b, h, i: (b, 0, 0)),
        pl.BlockSpec((None, None, V_ROWS, n_keys), lambda b, h, i: (b, h, 0, 0)),
    ]
    return pl.pallas_call(
        functools.partial(_attn_kernel, tq=tq, tk=tk, n_keys=n_keys, lam_init=lam_init),
        grid=(bsz, DIFF_HEADS, n_q // tq),
        in_specs=in_specs,
        out_specs=pl.BlockSpec((None, DIFF_DV, tq), lambda b, h, i: (b, h, i)),
        out_shape=jax.ShapeDtypeStruct((bsz, GROUP_W, n_q), BF16),
        scratch_shapes=[pltpu.VMEM((2, GROUP_W, tq), BF16),
                        pltpu.VMEM((2, tk, tq), BF16), pltpu.VMEM((2, tk, tq), BF16),
                        pltpu.VMEM((2, tk, tq), BF16), pltpu.VMEM((2, tk, tq), BF16)],
        compiler_params=pltpu.CompilerParams(
            dimension_semantics=("arbitrary", "arbitrary", "arbitrary"), vmem_limit_bytes=VMEM_LIMIT),
        name="diff_attn",
    )(*lam_params, subln_col, qt, k, vt)


def _outmlp_kernel(x_ref, yacd_ref, yt_ref, mod_ref, g2_ref, woacd_ref, wob_ref, w1_ref, w2_ref, fg_ref,
                   o_ref, *, ff_chunk, final_norm):
    gate_mix = mod_ref[2:3, :]
    shift, scale, gate_mlp = mod_ref[3:4, :], mod_ref[4:5, :], mod_ref[5:6, :]
    mix = jnp.dot(yacd_ref[...], woacd_ref[...], preferred_element_type=F32)
    mix = mix + lax.dot_general(yt_ref[...], wob_ref[...], (((0,), (0,)), ((), ())),
                                preferred_element_type=F32)
    x1 = x_ref[...] + gate_mix * mix
    h = _rms_mod(x1, g2_ref[...], shift, scale).astype(BF16)
    d_ff = w1_ref.shape[1]
    mlp = jnp.zeros_like(x1)
    for j in range(d_ff // ff_chunk):
        t = jnp.dot(h, w1_ref[:, j * ff_chunk:(j + 1) * ff_chunk], preferred_element_type=F32)
        t = jnp.square(jnp.maximum(t, 0.0)).astype(BF16)
        mlp = mlp + jnp.dot(t, w2_ref[j * ff_chunk:(j + 1) * ff_chunk, :], preferred_element_type=F32)
    x2 = x1 + gate_mlp * mlp
    if final_norm:
        x2 = x2 * lax.rsqrt(jnp.mean(x2 * x2, axis=-1, keepdims=True) + EPS) * fg_ref[...]
    o_ref[...] = x2


def _outmlp(x, yacd, yt, mod, mod_row, g2, woacd, wob, w1, w2, fg, tm, final_norm):
    bsz, n_tok, d = x.shape
    if mod_row is None:
        mod_map = lambda b, i: (b, 0, 0)
    else:
        mod_map = lambda b, i: (mod_row, 0, 0)
    const2 = lambda b, i: (0, 0)
    return pl.pallas_call(
        functools.partial(_outmlp_kernel, ff_chunk=1024, final_norm=final_norm),
        grid=(bsz, n_tok // tm),
        in_specs=[
            pl.BlockSpec((None, tm, d), lambda b, i: (b, i, 0)),
            pl.BlockSpec((None, tm, 3 * GROUP_W), lambda b, i: (b, i, 0)),
            pl.BlockSpec((None, GROUP_W, tm), lambda b, i: (b, 0, i)),
            pl.BlockSpec((None, 6, d), mod_map),
            pl.BlockSpec((1, d), const2),
            pl.BlockSpec(woacd.shape, const2),
            pl.BlockSpec(wob.shape, const2),
            pl.BlockSpec(w1.shape, const2),
            pl.BlockSpec(w2.shape, const2),
            pl.BlockSpec((1, d), const2),
        ],
        out_specs=pl.BlockSpec((None, tm, d), lambda b, i: (b, i, 0)),
        out_shape=jax.ShapeDtypeStruct((bsz, n_tok, d), F32),
        compiler_params=pltpu.CompilerParams(
            dimension_semantics=("arbitrary", "arbitrary"), vmem_limit_bytes=VMEM_LIMIT),
        name="outmlp",
    )(x, yacd, yt, mod, g2, woacd, wob, w1, w2, fg)


def _rope_tables(n_tok, identity):
    if identity:
        cos_r = cos_c = jnp.ones((n_tok, ROPE_FREQS), F32)
        sin_r = sin_c = jnp.zeros((n_tok, ROPE_FREQS), F32)
    else:
        rows = n_tok // GRID_W
        row = jnp.repeat(jnp.arange(rows, dtype=F32), GRID_W)
        col = jnp.tile(jnp.arange(GRID_W, dtype=F32), rows)
        inv = ROPE_BASE ** (-jnp.arange(0, ROPE_HALF, 2, dtype=F32) / ROPE_HALF)
        ang_r = row[:, None] * inv
        ang_c = col[:, None] * inv
        cos_r, sin_r, cos_c, sin_c = jnp.cos(ang_r), jnp.sin(ang_r), jnp.cos(ang_c), jnp.sin(ang_c)
    zero = jnp.zeros_like(sin_r)
    reps = GROUP_W // DIFF_DQK
    cos_n = jnp.tile(jnp.concatenate([cos_r, cos_r, cos_c, cos_c], axis=1), (1, reps))
    sina_n = jnp.tile(jnp.concatenate([-sin_r, zero, -sin_c, zero], axis=1), (1, reps))
    sinb_n = jnp.tile(jnp.concatenate([zero, sin_r, zero, sin_c], axis=1), (1, reps))
    cos_t = jnp.concatenate([cos_r, cos_c], axis=1).T
    sin_t = jnp.concatenate([sin_r, sin_c], axis=1).T
    return cos_n, sina_n, sinb_n, cos_t, sin_t


def kernel(x, c, ctx, c_ctx, ada_w, ada_b, norm1_g, norm2_g, w_in, conv_a_w, conv_a_b, ln_a_g, ln_a_b,
           lam_q1, lam_k1, lam_q2, lam_k2, subln_g, sg_ln_g, sg_ln_b, sg_w, sg_b, conv_d_w, w_out,
           mlp_w1, mlp_w2, final_g):
    bsz, n_tok, d = x.shape
    n_ctx = ctx.shape[1]
    depth = ada_w.shape[0]
    tm = min(512, n_tok)
    tm_ctx = min(512, n_ctx)
    ctx_row = bsz

    cc = jnp.zeros((8, d), F32).at[:bsz].set(c).at[ctx_row].set(c_ctx)
    mod = _modulation(cc, ada_w, ada_b)

    rope_x = _rope_tables(n_tok, identity=False)
    rope_c = _rope_tables(n_ctx, identity=True)
    row2 = lambda a: a.reshape(1, -1)

    h_ctx = ctx
    for l in range(depth):
        lam_init = 0.8 - 0.6 * math.exp(-0.3 * l)
        parts = [w_in[l][:, p * GROUP_W:(p + 1) * GROUP_W] for p in range(10)]
        wh = jnp.concatenate([parts[0], parts[1], parts[8], parts[9]], axis=1).astype(BF16)
        wr = jnp.concatenate([parts[3], parts[5], parts[6], parts[7]], axis=1).astype(BF16)
        wt = jnp.concatenate([parts[2], parts[4]], axis=1).T.astype(BF16)
        sgbias = jnp.repeat(sg_b[l].T, SG_DIM, axis=1)
        mix_w = (row2(norm1_g[l]), wh, wr, wt, conv_a_w[l], row2(conv_a_b[l]), row2(ln_a_g[l]),
                 row2(ln_a_b[l]), row2(sg_ln_g[l]), row2(sg_ln_b[l]), sg_w[l].astype(BF16), sgbias,
                 conv_d_w[l])
        lam_params = (row2(lam_q1[l]), row2(lam_k1[l]), row2(lam_q2[l]), row2(lam_k2[l]))
        subln_col = subln_g[l].reshape(DIFF_DV, 1)
        woacd = jnp.concatenate([w_out[l][0:GROUP_W], w_out[l][2 * GROUP_W:]], axis=0).astype(BF16)
        wob = w_out[l][GROUP_W:2 * GROUP_W].astype(BF16)
        w1 = mlp_w1[l].astype(BF16)
        w2 = mlp_w2[l].astype(BF16)
        g2 = row2(norm2_g[l])
        fg = row2(final_g)
        last = l == depth - 1

        yacd_c, k_c, qt_c, vt_c = _inmix(h_ctx, mod[l], ctx_row, *mix_w, rope_c, tm_ctx)
        yacd_x, k_x, qt_x, vt_x = _inmix(x, mod[l], None, *mix_w, rope_x, tm)
        k_all = jnp.concatenate([k_x, k_c], axis=1)
        vt_all = jnp.concatenate([vt_x, vt_c], axis=3)
        yt_x = _attention(lam_params, subln_col, qt_x, k_all, vt_all, lam_init)
        x = _outmlp(x, yacd_x, yt_x, mod[l], None, g2, woacd, wob, w1, w2, fg, tm, final_norm=last)
        if not last:
            yt_c = _attention(lam_params, subln_col, qt_c, k_c, vt_c, lam_init)
            h_ctx = _outmlp(h_ctx, yacd_c, yt_c, mod[l], ctx_row, g2, woacd, wob, w1, w2, fg, tm_ctx,
                            final_norm=False)
    return x
```

```python
import functools
import math

import jax
import jax.numpy as jnp
from jax import lax
from jax.experimental import pallas as pl
from jax.experimental.pallas import tpu as pltpu

F32 = jnp.float32
BF16 = jnp.bfloat16

GRID_W = 64
GROUP_W = 256
CONV_A_K = 31
CONV_D_K = 3
DIFF_HEADS = 4
DIFF_DQK = 32
DIFF_DV = 64
ROPE_HALF = DIFF_DQK // 2
ROPE_FREQS = ROPE_HALF // 2
ROPE_BASE = 10000.0
CHUNK = 128
SG_GROUPS = 4
SG_DIM = GROUP_W // SG_GROUPS
EPS = 1e-6

SUBLANES = 8
HALO = 16
V_ROWS = 80
TRIPS_PER_ITER = 16
NEG_BIG = -1e30
Q_SCALE = DIFF_DQK ** -0.5 * math.log2(math.e)
VMEM_LIMIT = 56 * 1024 * 1024


def _silu(x):
    return x * jax.nn.sigmoid(x)


def _layer_norm(x, g, b):
    xc = x - jnp.mean(x, axis=-1, keepdims=True)
    y = xc * lax.rsqrt(jnp.mean(xc * xc, axis=-1, keepdims=True) + EPS)
    return y * g + b


def _rms_mod(x, g, shift, scale):
    y = x * lax.rsqrt(jnp.mean(x * x, axis=-1, keepdims=True) + EPS)
    return (y * g) * (1.0 + scale) + shift


def _mod_kernel(cc_ref, w_ref, b_ref, o_ref):
    o_ref[...] = jnp.dot(_silu(cc_ref[...]), w_ref[...], preferred_element_type=F32) + b_ref[...]


def _modulation(cc, ada_w, ada_b):
    depth, d, d6 = ada_w.shape
    out = pl.pallas_call(
        _mod_kernel,
        grid=(depth, d6 // d),
        in_specs=[pl.BlockSpec((8, d), lambda l, j: (0, 0)),
                  pl.BlockSpec((None, d, d), lambda l, j: (l, 0, j)),
                  pl.BlockSpec((None, 1, d), lambda l, j: (l, 0, j))],
        out_specs=pl.BlockSpec((None, 8, d), lambda l, j: (l, 0, j)),
        out_shape=jax.ShapeDtypeStruct((depth, 8, d6), F32),
        name="ada_mod",
    )(cc, ada_w, ada_b.reshape(depth, 1, d6))
    return out.reshape(depth, 8, d6 // d, d)


def _inmix_kernel(x_ref, xp_ref, xn_ref, mod_ref, g1_ref, wh_ref, wr_ref, wt_ref,
                  caw_ref, cab_ref, lag_ref, lab_ref, sgg_ref, sgb_ref, sgw_ref, sgbias_ref, cdw_ref,
                  cos_ref, sina_ref, sinb_ref, cost_ref, sint_ref,
                  yacd_ref, k_ref, qt_ref, vt_ref, *, tm, n_tok):
    i = pl.program_id(1)
    shift = mod_ref[0:1, :]
    scale = mod_ref[1:2, :]
    g1 = g1_ref[...]
    h = _rms_mod(x_ref[...], g1, shift, scale).astype(BF16)
    hp = _rms_mod(xp_ref[...], g1, shift, scale).astype(BF16)
    hn = _rms_mod(xn_ref[...], g1, shift, scale).astype(BF16)
    h_ext = jnp.concatenate([hp, h, hn], axis=0)

    zh = jnp.dot(h_ext, wh_ref[...], preferred_element_type=F32)
    zr = jnp.dot(h, wr_ref[...], preferred_element_type=F32)
    zt = lax.dot_general(wt_ref[...], h, (((1,), (1,)), ((), ())),
                         preferred_element_type=F32)

    pos = i * tm - HALO + lax.broadcasted_iota(jnp.int32, (tm + 2 * HALO, 1), 0)
    valid = jnp.logical_and(pos >= 0, pos < n_tok)
    a_val, a_gate = zh[:, 0:GROUP_W], zh[:, GROUP_W:2 * GROUP_W]
    cg, xin = zh[:, 2 * GROUP_W:3 * GROUP_W], zh[:, 3 * GROUP_W:4 * GROUP_W]
    glu = jnp.where(valid, a_val * jax.nn.sigmoid(a_gate), 0.0)
    p = jnp.where(valid, cg * xin, 0.0)

    def depthwise(ext, w_ref, n_taps):
        rows = ext.shape[0]
        first = HALO - n_taps // 2
        out = jnp.zeros((tm, ext.shape[1]), F32)
        for r in range(SUBLANES):
            taps = [j for j in range(n_taps) if (first + j) % SUBLANES == r]
            if not taps:
                continue
            rolled = ext if r == 0 else pltpu.roll(ext, rows - r, 0)
            for j in taps:
                base = first + j - r
                out = out + w_ref[j:j + 1, :] * rolled[base:base + tm, :]
        return out

    y_a = _silu(_layer_norm(depthwise(glu, caw_ref, CONV_A_K) + cab_ref[...], lag_ref[...], lab_ref[...]))

    y_d = zr[:, 3 * GROUP_W:4 * GROUP_W] * depthwise(p, cdw_ref, CONV_D_K)

    u = jax.nn.gelu(zr[:, GROUP_W:2 * GROUP_W])
    sv = _layer_norm(jax.nn.gelu(zr[:, 2 * GROUP_W:3 * GROUP_W]), sgg_ref[...], sgb_ref[...]).astype(BF16)
    lane_group = lax.broadcasted_iota(jnp.int32, (CHUNK, GROUP_W), 1) // SG_DIM
    gated = []
    for c in range(tm // CHUNK):
        vc = sv[c * CHUNK:(c + 1) * CHUNK, :]
        s = sgbias_ref[...]
        for g in range(SG_GROUPS):
            sg = jnp.dot(sgw_ref[g], vc, preferred_element_type=F32)
            s = s + jnp.where(lane_group == g, sg, 0.0)
        gated.append(s)
    y_c = u * jnp.concatenate(gated, axis=0)

    yacd_ref[...] = jnp.concatenate([y_a, y_c, y_d], axis=-1).astype(BF16)

    k = zr[:, 0:GROUP_W]
    k = (k * cos_ref[...] + pltpu.roll(k, GROUP_W - ROPE_FREQS, 1) * sina_ref[...]
         + pltpu.roll(k, ROPE_FREQS, 1) * sinb_ref[...])
    k_ref[...] = k.astype(BF16)

    qt = zt[0:GROUP_W, :]
    cos_t, sin_t = cost_ref[...], sint_ref[...]
    rows = []
    for grp in range(GROUP_W // ROPE_HALF):
        lo = qt[grp * ROPE_HALF:grp * ROPE_HALF + ROPE_FREQS, :]
        hi = qt[grp * ROPE_HALF + ROPE_FREQS:(grp + 1) * ROPE_HALF, :]
        axis = grp % 2
        cs = cos_t[axis * ROPE_FREQS:(axis + 1) * ROPE_FREQS, :]
        sn = sin_t[axis * ROPE_FREQS:(axis + 1) * ROPE_FREQS, :]
        rows.append(lo * cs - hi * sn)
        rows.append(lo * sn + hi * cs)
    qt_ref[...] = (jnp.concatenate(rows, axis=0) * Q_SCALE).astype(BF16)

    pad_rows = jnp.where(lax.broadcasted_iota(jnp.int32, (V_ROWS - DIFF_DV, tm), 0) == 0, 1.0, 0.0)
    pieces = []
    for hd in range(DIFF_HEADS):
        pieces.append(zt[GROUP_W + hd * DIFF_DV:GROUP_W + (hd + 1) * DIFF_DV, :])
        pieces.append(pad_rows)
    vt_ref[...] = jnp.concatenate(pieces, axis=0).astype(BF16)


def _inmix(x, mod, mod_row, g1, wh, wr, wt, caw, cab, lag, lab, sgg, sgb, sgw, sgbias, cdw,
           rope_tabs, tm):
    bsz, n_tok, d = x.shape
    nt = n_tok // tm
    hb = tm // HALO
    n_hblk = n_tok // HALO
    cos_n, sina_n, sinb_n, cos_t, sin_t = rope_tabs
    if mod_row is None:
        mod_map = lambda b, i: (b, 0, 0)
    else:
        mod_map = lambda b, i: (mod_row, 0, 0)
    const2 = lambda b, i: (0, 0)
    tok_map = lambda b, i: (i, 0)
    in_specs = [
        pl.BlockSpec((None, tm, d), lambda b, i: (b, i, 0)),
        pl.BlockSpec((None, HALO, d), lambda b, i: (b, jnp.maximum(i * hb - 1, 0), 0)),
        pl.BlockSpec((None, HALO, d), lambda b, i: (b, jnp.minimum((i + 1) * hb, n_hblk - 1), 0)),
        pl.BlockSpec((None, 6, d), mod_map),
        pl.BlockSpec((1, d), const2),
        pl.BlockSpec(wh.shape, const2),
        pl.BlockSpec(wr.shape, const2),
        pl.BlockSpec(wt.shape, const2),
        pl.BlockSpec(caw.shape, const2),
        pl.BlockSpec(cab.shape, const2),
        pl.BlockSpec(lag.shape, const2),
        pl.BlockSpec(lab.shape, const2),
        pl.BlockSpec(sgg.shape, const2),
        pl.BlockSpec(sgb.shape, const2),
        pl.BlockSpec(sgw.shape, lambda b, i: (0, 0, 0)),
        pl.BlockSpec(sgbias.shape, const2),
        pl.BlockSpec(cdw.shape, const2),
        pl.BlockSpec((tm, GROUP_W), tok_map),
        pl.BlockSpec((tm, GROUP_W), tok_map),
        pl.BlockSpec((tm, GROUP_W), tok_map),
        pl.BlockSpec((2 * ROPE_FREQS, tm), lambda b, i: (0, i)),
        pl.BlockSpec((2 * ROPE_FREQS, tm), lambda b, i: (0, i)),
    ]
    out_specs = [
        pl.BlockSpec((None, tm, 3 * GROUP_W), lambda b, i: (b, i, 0)),
        pl.BlockSpec((None, tm, GROUP_W), lambda b, i: (b, i, 0)),
        pl.BlockSpec((None, GROUP_W, tm), lambda b, i: (b, 0, i)),
        pl.BlockSpec((None, DIFF_HEADS * V_ROWS, tm), lambda b, i: (b, 0, i)),
    ]
    out_shape = [
        jax.ShapeDtypeStruct((bsz, n_tok, 3 * GROUP_W), BF16),
        jax.ShapeDtypeStruct((bsz, n_tok, GROUP_W), BF16),
        jax.ShapeDtypeStruct((bsz, GROUP_W, n_tok), BF16),
        jax.ShapeDtypeStruct((bsz, DIFF_HEADS * V_ROWS, n_tok), BF16),
    ]
    yacd, k, qt, vt = pl.pallas_call(
        functools.partial(_inmix_kernel, tm=tm, n_tok=n_tok),
        grid=(bsz, nt),
        in_specs=in_specs,
        out_specs=out_specs,
        out_shape=out_shape,
        compiler_params=pltpu.CompilerParams(
            dimension_semantics=("arbitrary", "arbitrary"), vmem_limit_bytes=VMEM_LIMIT),
        name="inmix",
    )(x, x, x, mod, g1, wh, wr, wt, caw, cab, lag, lab, sgg, sgb, sgw, sgbias, cdw,
      cos_n, sina_n, sinb_n, cos_t, sin_t)
    return yacd, k, qt, vt.reshape(bsz, DIFF_HEADS, V_ROWS, n_tok)


def _attn_kernel(lq1_ref, lk1_ref, lq2_ref, lk2_ref, g_ref, qt_ref, k_ref, vt_ref, o_ref,
                 qm_ref, s0_ref, s1_ref, e0_ref, e1_ref, *, tq, tk, n_keys, lam_init):
    hd = pl.program_id(1)
    n_tiles = n_keys // tk
    s_slots, e_slots = (s0_ref, s1_ref), (e0_ref, e1_ref)

    qt = qt_ref[...]
    comp = lax.broadcasted_iota(jnp.int32, (GROUP_W, 1), 0) // DIFF_DQK
    for c in range(2):
        qm_ref[c] = jnp.where(comp == 2 * hd + c, qt, jnp.zeros_like(qt))

    def tile_start(t):
        return t * tk if isinstance(t, int) else pl.multiple_of(t * tk, tk)

    def trip(tau, par, carry):
        m, alpha, acc, mx = carry
        static = isinstance(tau, int)
        do_scores = (not static) or tau < n_tiles
        do_softmax = (not static) or 1 <= tau <= n_tiles
        do_values = (not static) or 2 <= tau <= n_tiles + 1
        new_m, new_alpha, new_acc, new_mx = list(m), list(alpha), list(acc), list(mx)
        if do_values:
            vt = vt_ref[:, pl.ds(tile_start(tau - 2), tk)]
            for c in range(2):
                pv = jnp.dot(vt, e_slots[par][c], preferred_element_type=F32)
                new_acc[c] = alpha[c] * acc[c] + pv
        if do_softmax:
            for c in range(2):
                m_new = jnp.maximum(m[c], mx[c])
                new_alpha[c] = jnp.exp2(m[c] - m_new)
                new_m[c] = m_new
                e_slots[1 - par][c] = jnp.exp2(s_slots[1 - par][c] - m_new.astype(BF16))
        if do_scores:
            kt = k_ref[pl.ds(tile_start(tau), tk), :]
            for c in range(2):
                s = jnp.dot(kt, qm_ref[c], preferred_element_type=F32).astype(BF16)
                new_mx[c] = jnp.max(s, axis=0, keepdims=True).astype(F32)
                s_slots[par][c] = s
        return tuple(new_m), tuple(new_alpha), tuple(new_acc), tuple(new_mx)

    row = lambda v: (jnp.full((1, tq), v, F32),) * 2
    carry = (row(NEG_BIG), row(1.0), (jnp.zeros((V_ROWS, tq), F32),) * 2, row(NEG_BIG))
    n_steady = max(n_tiles - 2, 0)
    for tau in range(min(2, n_tiles + 2)):
        carry = trip(tau, tau % 2, carry)

    def group(j, carry):
        for u in range(TRIPS_PER_ITER):
            carry = trip(2 + TRIPS_PER_ITER * j + u, u % 2, carry)
        return carry

    n_groups = n_steady // TRIPS_PER_ITER
    carry = lax.fori_loop(0, n_groups, group, carry)
    for tau in range(2 + TRIPS_PER_ITER * n_groups, n_tiles + 2):
        carry = trip(tau, tau % 2, carry)

    lam = (jnp.exp(jnp.sum(lq1_ref[...] * lk1_ref[...], axis=-1, keepdims=True))
           - jnp.exp(jnp.sum(lq2_ref[...] * lk2_ref[...], axis=-1, keepdims=True)) + lam_init)
    acc1, acc2 = carry[2]
    o1 = acc1[0:DIFF_DV, :] / acc1[DIFF_DV:DIFF_DV + 1, :]
    o2 = acc2[0:DIFF_DV, :] / acc2[DIFF_DV:DIFF_DV + 1, :]
    o = o1 - lam * o2
    y = o * lax.rsqrt(jnp.mean(o * o, axis=0, keepdims=True) + EPS)
    o_ref[...] = ((y * g_ref[...]) * (1.0 - lam_init)).astype(o_ref.dtype)


def _key_tile(n_keys):
    for tk in (768, 512, 256):
        if n_keys % tk == 0:
            return tk
    raise ValueError(f"key count {n_keys} is not a multiple of 256")


def _attention(lam_params, subln_col, qt, k, vt, lam_init):
    bsz, _, n_q = qt.shape
    n_keys = k.shape[1]
    tq, tk = min(1024, n_q), _key_tile(n_keys)
    const2 = lambda b, h, i: (0, 0)
    in_specs = [pl.BlockSpec((1, DIFF_DQK), const2)] * 4 + [
        pl.BlockSpec((DIFF_DV, 1), const2),
        pl.BlockSpec((None, GROUP_W, tq), lambda b, h, i: (b, 0, i)),
        pl.BlockSpec((None, n_keys, GROUP_W), lambda b, h, i: (b, 0, 0)),
        pl.BlockSpec((None, None, V_ROWS, n_keys), lambda b, h, i: (b, h, 0, 0)),
    ]
    return pl.pallas_call(
        functools.partial(_attn_kernel, tq=tq, tk=tk, n_keys=n_keys, lam_init=lam_init),
        grid=(bsz, DIFF_HEADS, n_q // tq),
        in_specs=in_specs,
        out_specs=pl.BlockSpec((None, DIFF_DV, tq), lambda b, h, i: (b, h, i)),
        out_shape=jax.ShapeDtypeStruct((bsz, GROUP_W, n_q), BF16),
        scratch_shapes=[pltpu.VMEM((2, GROUP_W, tq), BF16),
                        pltpu.VMEM((2, tk, tq), BF16), pltpu.VMEM((2, tk, tq), BF16),
                        pltpu.VMEM((2, tk, tq), BF16), pltpu.VMEM((2, tk, tq), BF16)],
        compiler_params=pltpu.CompilerParams(
            dimension_semantics=("arbitrary", "arbitrary", "arbitrary"), vmem_limit_bytes=VMEM_LIMIT),
        name="diff_attn",
    )(*lam_params, subln_col, qt, k, vt)


def _outmlp_kernel(x_ref, yacd_ref, yt_ref, mod_ref, g2_ref, woacd_ref, wob_ref, w1_ref, w2_ref, fg_ref,
                   o_ref, *, ff_chunk, final_norm):
    gate_mix = mod_ref[2:3, :]
    shift, scale, gate_mlp = mod_ref[3:4, :], mod_ref[4:5, :], mod_ref[5:6, :]
    mix = jnp.dot(yacd_ref[...], woacd_ref[...], preferred_element_type=F32)
    mix = mix + lax.dot_general(yt_ref[...], wob_ref[...], (((0,), (0,)), ((), ())),
                                preferred_element_type=F32)
    x1 = x_ref[...] + gate_mix * mix
    h = _rms_mod(x1, g2_ref[...], shift, scale).astype(BF16)
    d_ff = w1_ref.shape[1]
    mlp = jnp.zeros_like(x1)
    for j in range(d_ff // ff_chunk):
        t = jnp.dot(h, w1_ref[:, j * ff_chunk:(j + 1) * ff_chunk], preferred_element_type=F32)
        t = jnp.square(jnp.maximum(t, 0.0)).astype(BF16)
        mlp = mlp + jnp.dot(t, w2_ref[j * ff_chunk:(j + 1) * ff_chunk, :], preferred_element_type=F32)
    x2 = x1 + gate_mlp * mlp
    if final_norm:
        x2 = x2 * lax.rsqrt(jnp.mean(x2 * x2, axis=-1, keepdims=True) + EPS) * fg_ref[...]
    o_ref[...] = x2


def _outmlp(x, yacd, yt, mod, mod_row, g2, woacd, wob, w1, w2, fg, tm, final_norm):
    bsz, n_tok, d = x.shape
    if mod_row is None:
        mod_map = lambda b, i: (b, 0, 0)
    else:
        mod_map = lambda b, i: (mod_row, 0, 0)
    const2 = lambda b, i: (0, 0)
    return pl.pallas_call(
        functools.partial(_outmlp_kernel, ff_chunk=1024, final_norm=final_norm),
        grid=(bsz, n_tok // tm),
        in_specs=[
            pl.BlockSpec((None, tm, d), lambda b, i: (b, i, 0)),
            pl.BlockSpec((None, tm, 3 * GROUP_W), lambda b, i: (b, i, 0)),
            pl.BlockSpec((None, GROUP_W, tm), lambda b, i: (b, 0, i)),
            pl.BlockSpec((None, 6, d), mod_map),
            pl.BlockSpec((1, d), const2),
            pl.BlockSpec(woacd.shape, const2),
            pl.BlockSpec(wob.shape, const2),
            pl.BlockSpec(w1.shape, const2),
            pl.BlockSpec(w2.shape, const2),
            pl.BlockSpec((1, d), const2),
        ],
        out_specs=pl.BlockSpec((None, tm, d), lambda b, i: (b, i, 0)),
        out_shape=jax.ShapeDtypeStruct((bsz, n_tok, d), F32),
        compiler_params=pltpu.CompilerParams(
            dimension_semantics=("arbitrary", "arbitrary"), vmem_limit_bytes=VMEM_LIMIT),
        name="outmlp",
    )(x, yacd, yt, mod, g2, woacd, wob, w1, w2, fg)


def _rope_tables(n_tok, identity):
    if identity:
        cos_r = cos_c = jnp.ones((n_tok, ROPE_FREQS), F32)
        sin_r = sin_c = jnp.zeros((n_tok, ROPE_FREQS), F32)
    else:
        rows = n_tok // GRID_W
        row = jnp.repeat(jnp.arange(rows, dtype=F32), GRID_W)
        col = jnp.tile(jnp.arange(GRID_W, dtype=F32), rows)
        inv = ROPE_BASE ** (-jnp.arange(0, ROPE_HALF, 2, dtype=F32) / ROPE_HALF)
        ang_r = row[:, None] * inv
        ang_c = col[:, None] * inv
        cos_r, sin_r, cos_c, sin_c = jnp.cos(ang_r), jnp.sin(ang_r), jnp.cos(ang_c), jnp.sin(ang_c)
    zero = jnp.zeros_like(sin_r)
    reps = GROUP_W // DIFF_DQK
    cos_n = jnp.tile(jnp.concatenate([cos_r, cos_r, cos_c, cos_c], axis=1), (1, reps))
    sina_n = jnp.tile(jnp.concatenate([-sin_r, zero, -sin_c, zero], axis=1), (1, reps))
    sinb_n = jnp.tile(jnp.concatenate([zero, sin_r, zero, sin_c], axis=1), (1, reps))
    cos_t = jnp.concatenate([cos_r, cos_c], axis=1).T
    sin_t = jnp.concatenate([sin_r, sin_c], axis=1).T
    return cos_n, sina_n, sinb_n, cos_t, sin_t


def kernel(x, c, ctx, c_ctx, ada_w, ada_b, norm1_g, norm2_g, w_in, conv_a_w, conv_a_b, ln_a_g, ln_a_b,
           lam_q1, lam_k1, lam_q2, lam_k2, subln_g, sg_ln_g, sg_ln_b, sg_w, sg_b, conv_d_w, w_out,
           mlp_w1, mlp_w2, final_g):
    bsz, n_tok, d = x.shape
    n_ctx = ctx.shape[1]
    depth = ada_w.shape[0]
    tm = min(512, n_tok)
    tm_ctx = min(512, n_ctx)
    ctx_row = bsz

    cc = jnp.zeros((8, d), F32).at[:bsz].set(c).at[ctx_row].set(c_ctx)
    mod = _modulation(cc, ada_w, ada_b)

    rope_x = _rope_tables(n_tok, identity=False)
    rope_c = _rope_tables(n_ctx, identity=True)
    row2 = lambda a: a.reshape(1, -1)

    h_ctx = ctx
    for l in range(depth):
        lam_init = 0.8 - 0.6 * math.exp(-0.3 * l)
        parts = [w_in[l][:, p * GROUP_W:(p + 1) * GROUP_W] for p in range(10)]
        wh = jnp.concatenate([parts[0], parts[1], parts[8], parts[9]], axis=1).astype(BF16)
        wr = jnp.concatenate([parts[3], parts[5], parts[6], parts[7]], axis=1).astype(BF16)
        wt = jnp.concatenate([parts[2], parts[4]], axis=1).T.astype(BF16)
        sgbias = jnp.repeat(sg_b[l].T, SG_DIM, axis=1)
        mix_w = (row2(norm1_g[l]), wh, wr, wt, conv_a_w[l], row2(conv_a_b[l]), row2(ln_a_g[l]),
                 row2(ln_a_b[l]), row2(sg_ln_g[l]), row2(sg_ln_b[l]), sg_w[l].astype(BF16), sgbias,
                 conv_d_w[l])
        lam_params = (row2(lam_q1[l]), row2(lam_k1[l]), row2(lam_q2[l]), row2(lam_k2[l]))
        subln_col = subln_g[l].reshape(DIFF_DV, 1)
        woacd = jnp.concatenate([w_out[l][0:GROUP_W], w_out[l][2 * GROUP_W:]], axis=0).astype(BF16)
        wob = w_out[l][GROUP_W:2 * GROUP_W].astype(BF16)
        w1 = mlp_w1[l].astype(BF16)
        w2 = mlp_w2[l].astype(BF16)
        g2 = row2(norm2_g[l])
        fg = row2(final_g)
        last = l == depth - 1

        yacd_c, k_c, qt_c, vt_c = _inmix(h_ctx, mod[l], ctx_row, *mix_w, rope_c, tm_ctx)
        yacd_x, k_x, qt_x, vt_x = _inmix(x, mod[l], None, *mix_w, rope_x, tm)
        k_all = jnp.concatenate([k_x, k_c], axis=1)
        vt_all = jnp.concatenate([vt_x, vt_c], axis=3)
        yt_x = _attention(lam_params, subln_col, qt_x, k_all, vt_all, lam_init)
        x = _outmlp(x, yacd_x, yt_x, mod[l], None, g2, woacd, wob, w1, w2, fg, tm, final_norm=last)
        if not last:
            yt_c = _attention(lam_params, subln_col, qt_c, k_c, vt_c, lam_init)
            h_ctx = _outmlp(h_ctx, yacd_c, yt_c, mod[l], ctx_row, g2, woacd, wob, w1, w2, fg, tm_ctx,
                            final_norm=False)
    return x
```

```python
import functools
import math

import jax
import jax.numpy as jnp
from jax import lax
from jax.experimental import pallas as pl
from jax.experimental.pallas import tpu as pltpu

F32 = jnp.float32
BF16 = jnp.bfloat16

GRID_W = 64
GROUP_W = 256
CONV_A_K = 31
CONV_D_K = 3
DIFF_HEADS = 4
DIFF_DQK = 32
DIFF_DV = 64
HEAD_W = 2 * DIFF_DQK
ROPE_HALF = DIFF_DQK // 2
ROPE_FREQS = ROPE_HALF // 2
ROPE_BASE = 10000.0
CHUNK = 128
SG_GROUPS = 4
SG_DIM = GROUP_W // SG_GROUPS
EPS = 1e-6

SUBLANES = 8
HALO = 16
V_ROWS = 80
TRIPS_PER_ITER = 16
NEG_BIG = -1e30
Q_SCALE = DIFF_DQK ** -0.5 * math.log2(math.e)
VMEM_LIMIT = 56 * 1024 * 1024


def _silu(x):
    return x * jax.nn.sigmoid(x)


def _layer_norm(x, g, b):
    xc = x - jnp.mean(x, axis=-1, keepdims=True)
    y = xc * lax.rsqrt(jnp.mean(xc * xc, axis=-1, keepdims=True) + EPS)
    return y * g + b


def _rms_mod(x, g, shift, scale):
    y = x * lax.rsqrt(jnp.mean(x * x, axis=-1, keepdims=True) + EPS)
    return (y * g) * (1.0 + scale) + shift


def _mod_kernel(cc_ref, w_ref, b_ref, o_ref):
    o_ref[...] = jnp.dot(_silu(cc_ref[...]), w_ref[...], preferred_element_type=F32) + b_ref[...]


def _modulation(cc, ada_w, ada_b):
    depth, d, d6 = ada_w.shape
    out = pl.pallas_call(
        _mod_kernel,
        grid=(depth, d6 // d),
        in_specs=[pl.BlockSpec((8, d), lambda l, j: (0, 0)),
                  pl.BlockSpec((None, d, d), lambda l, j: (l, 0, j)),
                  pl.BlockSpec((None, 1, d), lambda l, j: (l, 0, j))],
        out_specs=pl.BlockSpec((None, 8, d), lambda l, j: (l, 0, j)),
        out_shape=jax.ShapeDtypeStruct((depth, 8, d6), F32),
        name="ada_mod",
    )(cc, ada_w, ada_b.reshape(depth, 1, d6))
    return out.reshape(depth, 8, d6 // d, d)


def _inmix_kernel(x_ref, xp_ref, xn_ref, mod_ref, g1_ref, wh_ref, wr_ref, wt_ref,
                  caw_ref, cab_ref, lag_ref, lab_ref, sgg_ref, sgb_ref, sgw_ref, sgbias_ref, cdw_ref,
                  cos_ref, sina_ref, sinb_ref, cost_ref, sint_ref,
                  yacd_ref, k_ref, qt_ref, vt_ref, *, tm, n_tok):
    i = pl.program_id(1)
    shift = mod_ref[0:1, :]
    scale = mod_ref[1:2, :]
    g1 = g1_ref[...]
    h = _rms_mod(x_ref[...], g1, shift, scale).astype(BF16)
    hp = _rms_mod(xp_ref[...], g1, shift, scale).astype(BF16)
    hn = _rms_mod(xn_ref[...], g1, shift, scale).astype(BF16)
    h_ext = jnp.concatenate([hp, h, hn], axis=0)

    zh = jnp.dot(h_ext, wh_ref[...], preferred_element_type=F32)
    zr = jnp.dot(h, wr_ref[...], preferred_element_type=F32)
    zt = lax.dot_general(wt_ref[...], h, (((1,), (1,)), ((), ())),
                         preferred_element_type=F32)

    pos = i * tm - HALO + lax.broadcasted_iota(jnp.int32, (tm + 2 * HALO, 1), 0)
    valid = jnp.logical_and(pos >= 0, pos < n_tok)
    a_val, a_gate = zh[:, 0:GROUP_W], zh[:, GROUP_W:2 * GROUP_W]
    cg, xin = zh[:, 2 * GROUP_W:3 * GROUP_W], zh[:, 3 * GROUP_W:4 * GROUP_W]
    glu = jnp.where(valid, a_val * jax.nn.sigmoid(a_gate), 0.0)
    p = jnp.where(valid, cg * xin, 0.0)

    def depthwise(ext, w_ref, n_taps):
        rows = ext.shape[0]
        first = HALO - n_taps // 2
        out = jnp.zeros((tm, ext.shape[1]), F32)
        for r in range(SUBLANES):
            taps = [j for j in range(n_taps) if (first + j) % SUBLANES == r]
            if not taps:
                continue
            rolled = ext if r == 0 else pltpu.roll(ext, rows - r, 0)
            for j in taps:
                base = first + j - r
                out = out + w_ref[j:j + 1, :] * rolled[base:base + tm, :]
        return out

    y_a = _silu(_layer_norm(depthwise(glu, caw_ref, CONV_A_K) + cab_ref[...], lag_ref[...], lab_ref[...]))

    y_d = zr[:, 3 * GROUP_W:4 * GROUP_W] * depthwise(p, cdw_ref, CONV_D_K)

    u = jax.nn.gelu(zr[:, GROUP_W:2 * GROUP_W])
    sv = _layer_norm(jax.nn.gelu(zr[:, 2 * GROUP_W:3 * GROUP_W]), sgg_ref[...], sgb_ref[...]).astype(BF16)
    lane_group = lax.broadcasted_iota(jnp.int32, (CHUNK, GROUP_W), 1) // SG_DIM
    gated = []
    for c in range(tm // CHUNK):
        vc = sv[c * CHUNK:(c + 1) * CHUNK, :]
        s = sgbias_ref[...]
        for g in range(SG_GROUPS):
            sg = jnp.dot(sgw_ref[g], vc, preferred_element_type=F32)
            s = s + jnp.where(lane_group == g, sg, 0.0)
        gated.append(s)
    y_c = u * jnp.concatenate(gated, axis=0)

    yacd_ref[...] = jnp.concatenate([y_a, y_c, y_d], axis=-1).astype(BF16)

    k = zr[:, 0:GROUP_W]
    k = (k * cos_ref[...] + pltpu.roll(k, GROUP_W - ROPE_FREQS, 1) * sina_ref[...]
         + pltpu.roll(k, ROPE_FREQS, 1) * sinb_ref[...])
    k = k.astype(BF16)
    for hd in range(DIFF_HEADS):
        k_ref[hd] = k[:, hd * HEAD_W:(hd + 1) * HEAD_W]

    qt = zt[0:GROUP_W, :]
    cos_t, sin_t = cost_ref[...], sint_ref[...]
    rows = []
    for grp in range(GROUP_W // ROPE_HALF):
        lo = qt[grp * ROPE_HALF:grp * ROPE_HALF + ROPE_FREQS, :]
        hi = qt[grp * ROPE_HALF + ROPE_FREQS:(grp + 1) * ROPE_HALF, :]
        axis = grp % 2
        cs = cos_t[axis * ROPE_FREQS:(axis + 1) * ROPE_FREQS, :]
        sn = sin_t[axis * ROPE_FREQS:(axis + 1) * ROPE_FREQS, :]
        rows.append(lo * cs - hi * sn)
        rows.append(lo * sn + hi * cs)
    qt_ref[...] = (jnp.concatenate(rows, axis=0) * Q_SCALE).astype(BF16)

    pad_rows = jnp.where(lax.broadcasted_iota(jnp.int32, (V_ROWS - DIFF_DV, tm), 0) == 0, 1.0, 0.0)
    pieces = []
    for hd in range(DIFF_HEADS):
        pieces.append(zt[GROUP_W + hd * DIFF_DV:GROUP_W + (hd + 1) * DIFF_DV, :])
        pieces.append(pad_rows)
    vt_ref[...] = jnp.concatenate(pieces, axis=0).astype(BF16)


def _inmix(x, mod, mod_row, g1, wh, wr, wt, caw, cab, lag, lab, sgg, sgb, sgw, sgbias, cdw,
           rope_tabs, tm):
    bsz, n_tok, d = x.shape
    nt = n_tok // tm
    hb = tm // HALO
    n_hblk = n_tok // HALO
    cos_n, sina_n, sinb_n, cos_t, sin_t = rope_tabs
    if mod_row is None:
        mod_map = lambda b, i: (b, 0, 0)
    else:
        mod_map = lambda b, i: (mod_row, 0, 0)
    const2 = lambda b, i: (0, 0)
    tok_map = lambda b, i: (i, 0)
    in_specs = [
        pl.BlockSpec((None, tm, d), lambda b, i: (b, i, 0)),
        pl.BlockSpec((None, HALO, d), lambda b, i: (b, jnp.maximum(i * hb - 1, 0), 0)),
        pl.BlockSpec((None, HALO, d), lambda b, i: (b, jnp.minimum((i + 1) * hb, n_hblk - 1), 0)),
        pl.BlockSpec((None, 6, d), mod_map),
        pl.BlockSpec((1, d), const2),
        pl.BlockSpec(wh.shape, const2),
        pl.BlockSpec(wr.shape, const2),
        pl.BlockSpec(wt.shape, const2),
        pl.BlockSpec(caw.shape, const2),
        pl.BlockSpec(cab.shape, const2),
        pl.BlockSpec(lag.shape, const2),
        pl.BlockSpec(lab.shape, const2),
        pl.BlockSpec(sgg.shape, const2),
        pl.BlockSpec(sgb.shape, const2),
        pl.BlockSpec(sgw.shape, lambda b, i: (0, 0, 0)),
        pl.BlockSpec(sgbias.shape, const2),
        pl.BlockSpec(cdw.shape, const2),
        pl.BlockSpec((tm, GROUP_W), tok_map),
        pl.BlockSpec((tm, GROUP_W), tok_map),
        pl.BlockSpec((tm, GROUP_W), tok_map),
        pl.BlockSpec((2 * ROPE_FREQS, tm), lambda b, i: (0, i)),
        pl.BlockSpec((2 * ROPE_FREQS, tm), lambda b, i: (0, i)),
    ]
    out_specs = [
        pl.BlockSpec((None, tm, 3 * GROUP_W), lambda b, i: (b, i, 0)),
        pl.BlockSpec((None, DIFF_HEADS, tm, HEAD_W), lambda b, i: (b, 0, i, 0)),
        pl.BlockSpec((None, GROUP_W, tm), lambda b, i: (b, 0, i)),
        pl.BlockSpec((None, DIFF_HEADS * V_ROWS, tm), lambda b, i: (b, 0, i)),
    ]
    out_shape = [
        jax.ShapeDtypeStruct((bsz, n_tok, 3 * GROUP_W), BF16),
        jax.ShapeDtypeStruct((bsz, DIFF_HEADS, n_tok, HEAD_W), BF16),
        jax.ShapeDtypeStruct((bsz, GROUP_W, n_tok), BF16),
        jax.ShapeDtypeStruct((bsz, DIFF_HEADS * V_ROWS, n_tok), BF16),
    ]
    yacd, k, qt, vt = pl.pallas_call(
        functools.partial(_inmix_kernel, tm=tm, n_tok=n_tok),
        grid=(bsz, nt),
        in_specs=in_specs,
        out_specs=out_specs,
        out_shape=out_shape,
        compiler_params=pltpu.CompilerParams(
            dimension_semantics=("arbitrary", "arbitrary"), vmem_limit_bytes=VMEM_LIMIT),
        name="inmix",
    )(x, x, x, mod, g1, wh, wr, wt, caw, cab, lag, lab, sgg, sgb, sgw, sgbias, cdw,
      cos_n, sina_n, sinb_n, cos_t, sin_t)
    return yacd, k, qt, vt.reshape(bsz, DIFF_HEADS, V_ROWS, n_tok)


def _attn_kernel(lq1_ref, lk1_ref, lq2_ref, lk2_ref, g_ref, qt_ref, k_ref, vt_ref, o_ref,
                 qm_ref, s0_ref, s1_ref, e0_ref, e1_ref, *, tq, tk, n_keys, lam_init):
    n_tiles = n_keys // tk
    s_slots, e_slots = (s0_ref, s1_ref), (e0_ref, e1_ref)

    qt = qt_ref[...]
    comp = lax.broadcasted_iota(jnp.int32, (HEAD_W, 1), 0) // DIFF_DQK
    for c in range(2):
        qm_ref[c] = jnp.where(comp == c, qt, jnp.zeros_like(qt))

    def tile_start(t):
        return t * tk if isinstance(t, int) else pl.multiple_of(t * tk, tk)

    def trip(tau, par, carry):
        m, alpha, acc, mx = carry
        static = isinstance(tau, int)
        do_scores = (not static) or tau < n_tiles
        do_softmax = (not static) or 1 <= tau <= n_tiles
        do_values = (not static) or 2 <= tau <= n_tiles + 1
        new_m, new_alpha, new_acc, new_mx = list(m), list(alpha), list(acc), list(mx)
        if do_values:
            vt = vt_ref[:, pl.ds(tile_start(tau - 2), tk)]
            for c in range(2):
                pv = jnp.dot(vt, e_slots[par][c], preferred_element_type=F32)
                new_acc[c] = alpha[c] * acc[c] + pv
        if do_softmax:
            for c in range(2):
                m_new = jnp.maximum(m[c], mx[c])
                new_alpha[c] = jnp.exp2(m[c] - m_new)
                new_m[c] = m_new
                e_slots[1 - par][c] = jnp.exp2(s_slots[1 - par][c] - m_new.astype(BF16))
        if do_scores:
            kt = k_ref[pl.ds(tile_start(tau), tk), :]
            for c in range(2):
                s = jnp.dot(kt, qm_ref[c], preferred_element_type=F32).astype(BF16)
                new_mx[c] = jnp.max(s, axis=0, keepdims=True).astype(F32)
                s_slots[par][c] = s
        return tuple(new_m), tuple(new_alpha), tuple(new_acc), tuple(new_mx)

    row = lambda v: (jnp.full((1, tq), v, F32),) * 2
    carry = (row(NEG_BIG), row(1.0), (jnp.zeros((V_ROWS, tq), F32),) * 2, row(NEG_BIG))
    n_steady = max(n_tiles - 2, 0)
    for tau in range(min(2, n_tiles + 2)):
        carry = trip(tau, tau % 2, carry)

    def group(j, carry):
        for u in range(TRIPS_PER_ITER):
            carry = trip(2 + TRIPS_PER_ITER * j + u, u % 2, carry)
        return carry

    n_groups = n_steady // TRIPS_PER_ITER
    carry = lax.fori_loop(0, n_groups, group, carry)
    for tau in range(2 + TRIPS_PER_ITER * n_groups, n_tiles + 2):
        carry = trip(tau, tau % 2, carry)

    lam = (jnp.exp(jnp.sum(lq1_ref[...] * lk1_ref[...], axis=-1, keepdims=True))
           - jnp.exp(jnp.sum(lq2_ref[...] * lk2_ref[...], axis=-1, keepdims=True)) + lam_init)
    acc1, acc2 = carry[2]
    o1 = acc1[0:DIFF_DV, :] / acc1[DIFF_DV:DIFF_DV + 1, :]
    o2 = acc2[0:DIFF_DV, :] / acc2[DIFF_DV:DIFF_DV + 1, :]
    o = o1 - lam * o2
    y = o * lax.rsqrt(jnp.mean(o * o, axis=0, keepdims=True) + EPS)
    o_ref[...] = ((y * g_ref[...]) * (1.0 - lam_init)).astype(o_ref.dtype)


def _key_tile(n_keys):
    for tk in (768, 512, 256):
        if n_keys % tk == 0:
            return tk
    raise ValueError(f"key count {n_keys} is not a multiple of 256")


def _attention(lam_params, subln_col, qt, k, vt, lam_init):
    bsz, _, n_q = qt.shape
    n_keys = k.shape[2]
    tq, tk = min(1024, n_q), _key_tile(n_keys)
    const2 = lambda b, h, i: (0, 0)
    in_specs = [pl.BlockSpec((1, DIFF_DQK), const2)] * 4 + [
        pl.BlockSpec((DIFF_DV, 1), const2),
        pl.BlockSpec((None, HEAD_W, tq), lambda b, h, i: (b, h, i)),
        pl.BlockSpec((None, None, n_keys, HEAD_W), lambda b, h, i: (b, h, 0, 0)),
        pl.BlockSpec((None, None, V_ROWS, n_keys), lambda b, h, i: (b, h, 0, 0)),
    ]
    return pl.pallas_call(
        functools.partial(_attn_kernel, tq=tq, tk=tk, n_keys=n_keys, lam_init=lam_init),
        grid=(bsz, DIFF_HEADS, n_q // tq),
        in_specs=in_specs,
        out_specs=pl.BlockSpec((None, DIFF_DV, tq), lambda b, h, i: (b, h, i)),
        out_shape=jax.ShapeDtypeStruct((bsz, GROUP_W, n_q), BF16),
        scratch_shapes=[pltpu.VMEM((2, HEAD_W, tq), BF16),
                        pltpu.VMEM((2, tk, tq), BF16), pltpu.VMEM((2, tk, tq), BF16),
                        pltpu.VMEM((2, tk, tq), BF16), pltpu.VMEM((2, tk, tq), BF16)],
        compiler_params=pltpu.CompilerParams(
            dimension_semantics=("arbitrary", "arbitrary", "arbitrary"), vmem_limit_bytes=VMEM_LIMIT),
        name="diff_attn",
    )(*lam_params, subln_col, qt, k, vt)


def _outmlp_kernel(x_ref, yacd_ref, yt_ref, mod_ref, g2_ref, woacd_ref, wob_ref, w1_ref, w2_ref, fg_ref,
                   o_ref, *, ff_chunk, final_norm):
    gate_mix = mod_ref[2:3, :]
    shift, scale, gate_mlp = mod_ref[3:4, :], mod_ref[4:5, :], mod_ref[5:6, :]
    mix = jnp.dot(yacd_ref[...], woacd_ref[...], preferred_element_type=F32)
    mix = mix + lax.dot_general(yt_ref[...], wob_ref[...], (((0,), (0,)), ((), ())),
                                preferred_element_type=F32)
    x1 = x_ref[...] + gate_mix * mix
    h = _rms_mod(x1, g2_ref[...], shift, scale).astype(BF16)
    d_ff = w1_ref.shape[1]
    mlp = jnp.zeros_like(x1)
    for j in range(d_ff // ff_chunk):
        t = jnp.dot(h, w1_ref[:, j * ff_chunk:(j + 1) * ff_chunk], preferred_element_type=F32)
        t = jnp.square(jnp.maximum(t, 0.0)).astype(BF16)
        mlp = mlp + jnp.dot(t, w2_ref[j * ff_chunk:(j + 1) * ff_chunk, :], preferred_element_type=F32)
    x2 = x1 + gate_mlp * mlp
    if final_norm:
        x2 = x2 * lax.rsqrt(jnp.mean(x2 * x2, axis=-1, keepdims=True) + EPS) * fg_ref[...]
    o_ref[...] = x2


def _outmlp(x, yacd, yt, mod, mod_row, g2, woacd, wob, w1, w2, fg, tm, final_norm):
    bsz, n_tok, d = x.shape
    if mod_row is None:
        mod_map = lambda b, i: (b, 0, 0)
    else:
        mod_map = lambda b, i: (mod_row, 0, 0)
    const2 = lambda b, i: (0, 0)
    return pl.pallas_call(
        functools.partial(_outmlp_kernel, ff_chunk=1024, final_norm=final_norm),
        grid=(bsz, n_tok // tm),
        in_specs=[
            pl.BlockSpec((None, tm, d), lambda b, i: (b, i, 0)),
            pl.BlockSpec((None, tm, 3 * GROUP_W), lambda b, i: (b, i, 0)),
            pl.BlockSpec((None, GROUP_W, tm), lambda b, i: (b, 0, i)),
            pl.BlockSpec((None, 6, d), mod_map),
            pl.BlockSpec((1, d), const2),
            pl.BlockSpec(woacd.shape, const2),
            pl.BlockSpec(wob.shape, const2),
            pl.BlockSpec(w1.shape, const2),
            pl.BlockSpec(w2.shape, const2),
            pl.BlockSpec((1, d), const2),
        ],
        out_specs=pl.BlockSpec((None, tm, d), lambda b, i: (b, i, 0)),
        out_shape=jax.ShapeDtypeStruct((bsz, n_tok, d), F32),
        compiler_params=pltpu.CompilerParams(
            dimension_semantics=("arbitrary", "arbitrary"), vmem_limit_bytes=VMEM_LIMIT),
        name="outmlp",
    )(x, yacd, yt, mod, g2, woacd, wob, w1, w2, fg)


def _rope_tables(n_tok, identity):
    if identity:
        cos_r = cos_c = jnp.ones((n_tok, ROPE_FREQS), F32)
        sin_r = sin_c = jnp.zeros((n_tok, ROPE_FREQS), F32)
    else:
        rows = n_tok // GRID_W
        row = jnp.repeat(jnp.arange(rows, dtype=F32), GRID_W)
        col = jnp.tile(jnp.arange(GRID_W, dtype=F32), rows)
        inv = ROPE_BASE ** (-jnp.arange(0, ROPE_HALF, 2, dtype=F32) / ROPE_HALF)
        ang_r = row[:, None] * inv
        ang_c = col[:, None] * inv
        cos_r, sin_r, cos_c, sin_c = jnp.cos(ang_r), jnp.sin(ang_r), jnp.cos(ang_c), jnp.sin(ang_c)
    zero = jnp.zeros_like(sin_r)
    reps = GROUP_W // DIFF_DQK
    cos_n = jnp.tile(jnp.concatenate([cos_r, cos_r, cos_c, cos_c], axis=1), (1, reps))
    sina_n = jnp.tile(jnp.concatenate([-sin_r, zero, -sin_c, zero], axis=1), (1, reps))
    sinb_n = jnp.tile(jnp.concatenate([zero, sin_r, zero, sin_c], axis=1), (1, reps))
    cos_t = jnp.concatenate([cos_r, cos_c], axis=1).T
    sin_t = jnp.concatenate([sin_r, sin_c], axis=1).T
    return cos_n, sina_n, sinb_n, cos_t, sin_t


def kernel(x, c, ctx, c_ctx, ada_w, ada_b, norm1_g, norm2_g, w_in, conv_a_w, conv_a_b, ln_a_g, ln_a_b,
           lam_q1, lam_k1, lam_q2, lam_k2, subln_g, sg_ln_g, sg_ln_b, sg_w, sg_b, conv_d_w, w_out,
           mlp_w1, mlp_w2, final_g):
    bsz, n_tok, d = x.shape
    n_ctx = ctx.shape[1]
    depth = ada_w.shape[0]
    tm = min(512, n_tok)
    tm_ctx = min(512, n_ctx)
    ctx_row = bsz

    cc = jnp.zeros((8, d), F32).at[:bsz].set(c).at[ctx_row].set(c_ctx)
    mod = _modulation(cc, ada_w, ada_b)

    rope_x = _rope_tables(n_tok, identity=False)
    rope_c = _rope_tables(n_ctx, identity=True)
    row2 = lambda a: a.reshape(1, -1)

    h_ctx = ctx
    for l in range(depth):
        lam_init = 0.8 - 0.6 * math.exp(-0.3 * l)
        parts = [w_in[l][:, p * GROUP_W:(p + 1) * GROUP_W] for p in range(10)]
        wh = jnp.concatenate([parts[0], parts[1], parts[8], parts[9]], axis=1).astype(BF16)
        wr = jnp.concatenate([parts[3], parts[5], parts[6], parts[7]], axis=1).astype(BF16)
        wt = jnp.concatenate([parts[2], parts[4]], axis=1).T.astype(BF16)
        sgbias = jnp.repeat(sg_b[l].T, SG_DIM, axis=1)
        mix_w = (row2(norm1_g[l]), wh, wr, wt, conv_a_w[l], row2(conv_a_b[l]), row2(ln_a_g[l]),
                 row2(ln_a_b[l]), row2(sg_ln_g[l]), row2(sg_ln_b[l]), sg_w[l].astype(BF16), sgbias,
                 conv_d_w[l])
        lam_params = (row2(lam_q1[l]), row2(lam_k1[l]), row2(lam_q2[l]), row2(lam_k2[l]))
        subln_col = subln_g[l].reshape(DIFF_DV, 1)
        woacd = jnp.concatenate([w_out[l][0:GROUP_W], w_out[l][2 * GROUP_W:]], axis=0).astype(BF16)
        wob = w_out[l][GROUP_W:2 * GROUP_W].astype(BF16)
        w1 = mlp_w1[l].astype(BF16)
        w2 = mlp_w2[l].astype(BF16)
        g2 = row2(norm2_g[l])
        fg = row2(final_g)
        last = l == depth - 1

        yacd_c, k_c, qt_c, vt_c = _inmix(h_ctx, mod[l], ctx_row, *mix_w, rope_c, tm_ctx)
        yacd_x, k_x, qt_x, vt_x = _inmix(x, mod[l], None, *mix_w, rope_x, tm)
        k_all = jnp.concatenate([k_x, k_c], axis=2)
        vt_all = jnp.concatenate([vt_x, vt_c], axis=3)
        yt_x = _attention(lam_params, subln_col, qt_x, k_all, vt_all, lam_init)
        x = _outmlp(x, yacd_x, yt_x, mod[l], None, g2, woacd, wob, w1, w2, fg, tm, final_norm=last)
        if not last:
            yt_c = _attention(lam_params, subln_col, qt_c, k_c, vt_c, lam_init)
            h_ctx = _outmlp(h_ctx, yacd_c, yt_c, mod[l], ctx_row, g2, woacd, wob, w1, w2, fg, tm_ctx,
                            final_norm=False)
    return x
```
